```python
import math
import jax, jax.numpy as jnp
from jax import lax
import numpy as np

D_MODEL = 1024
BATCH = 8
SEQ = 2048
DEPTH = 2

CHUNK = 64
NORM_EPS = 1e-6
FFN_HALF = 0.5
D_FF = 2816

A_HEAD_DIM = 64
A_WIDTH = D_MODEL // 2
A_HEADS = A_WIDTH // A_HEAD_DIM
W_LORA = 64
A_LORA = 64
G_LORA = 128
GN_EPS = 64e-5
A_PROJ = 3 * A_WIDTH + W_LORA + A_LORA + G_LORA
A_SPLITS = (A_WIDTH, 2 * A_WIDTH, 3 * A_WIDTH, 3 * A_WIDTH + W_LORA,
            3 * A_WIDTH + W_LORA + A_LORA)

B_WIDTH = D_MODEL // 2
B_BLOCKS = 8
B_BLOCK_DIM = B_WIDTH // B_BLOCKS
CONV_WIDTH = 4
LRU_C = 8.0

IN_PROJ = A_PROJ + 2 * B_WIDTH

C_HEAD_DIM = 64
C_HEADS = D_MODEL // C_HEAD_DIM
C_WIDTH = C_HEADS * C_HEAD_DIM
Q_BLOCK = 128

kernel_name = "hybrid_rwkv7_rglru_stickbreaking_macaron"


def rmsnorm(x, g):
    xf = x.astype(jnp.float32)
    y = xf * lax.rsqrt(jnp.mean(xf * xf, axis=-1, keepdims=True) + NORM_EPS)
    return (y * g.astype(jnp.float32)).astype(x.dtype)


def swiglu(x, w_in, w_out):
    gate, up = jnp.split(x @ w_in, 2, axis=-1)
    return (jax.nn.silu(gate) * up) @ w_out


def shift_right(x):
    return jnp.pad(x, ((0, 0), (1, 0), (0, 0)))[:, :-1]


def rwkv7_step(S, inp):
    r_t, w_t, k_t, v_t, kk_t, a_t = inp
    sa = jnp.einsum("bhvk,bhk->bhv", S, -kk_t)
    S = (S * w_t[:, :, None, :]
         + jnp.einsum("bhv,bhk->bhvk", sa, kk_t * a_t)
         + jnp.einsum("bhv,bhk->bhvk", v_t, k_t))
    y = jnp.einsum("bhvk,bhk->bhv", S, r_t)
    return S, y


def rwkv7_time_mix(pa, mu, w0, w2, a0, a2, g2, k_k, k_a, r_k, lnx_g, lnx_b):
    bsz, seqlen, _ = pa.shape
    dt = pa.dtype
    f32 = jnp.float32
    pa = pa + mu * (shift_right(pa) - pa)
    r, k, v, w_lo, a_lo, g_lo = jnp.split(pa, A_SPLITS, axis=-1)
    w_log = -jax.nn.softplus(-(w0 + jnp.tanh(w_lo) @ w2).astype(f32)) - 0.5
    decay = jnp.exp(-jnp.exp(w_log))
    a = jax.nn.sigmoid((a0 + a_lo @ a2).astype(f32))
    g = jax.nn.sigmoid(g_lo) @ g2

    def heads(t):
        return t.astype(f32).reshape(bsz, seqlen, A_HEADS, A_HEAD_DIM)

    kk = heads(k * k_k)
    kk = kk / jnp.maximum(jnp.sqrt(jnp.sum(kk * kk, axis=-1, keepdims=True)), 1e-12)
    a_h = heads(a)
    k_h = heads(k) * (1.0 + (a_h - 1.0) * k_a.astype(f32).reshape(A_HEADS, A_HEAD_DIM))
    r_h, v_h, w_h = heads(r), heads(v), heads(decay)

    S0 = jnp.zeros((bsz, A_HEADS, A_HEAD_DIM, A_HEAD_DIM), f32)
    xs = tuple(jnp.moveaxis(t, 1, 0) for t in (r_h, w_h, k_h, v_h, kk, a_h))
    _, y = lax.scan(rwkv7_step, S0, xs)
    y = jnp.moveaxis(y, 0, 1)

    mean = jnp.mean(y, axis=-1, keepdims=True)
    var = jnp.mean(jnp.square(y - mean), axis=-1, keepdims=True)
    yn = (y - mean) * lax.rsqrt(var + GN_EPS)
    yn = (yn * lnx_g.astype(f32).reshape(A_HEADS, A_HEAD_DIM)
          + lnx_b.astype(f32).reshape(A_HEADS, A_HEAD_DIM))
    bonus = jnp.sum(r_h * k_h * r_k.astype(f32), axis=-1, keepdims=True) * v_h
    return (yn + bonus).reshape(bsz, seqlen, A_WIDTH).astype(dt) * g


def lru_combine(c1, c2):
    a1, b1 = c1
    a2, b2 = c2
    return a1 * a2, a2 * b1 + b2


def rglru_mix(xb, gb, conv_w, conv_b, gate_a_w, gate_a_b, gate_x_w, gate_x_b, lam):
    bsz, seqlen, _ = xb.shape
    f32 = jnp.float32
    xp = jnp.pad(xb, ((0, 0), (CONV_WIDTH - 1, 0), (0, 0)))
    xc = conv_b + sum(xp[:, i:i + seqlen] * conv_w[i] for i in range(CONV_WIDTH))
    blocks = xc.reshape(bsz, seqlen, B_BLOCKS, B_BLOCK_DIM)
    r = jax.nn.sigmoid((jnp.einsum("btni,nij->btnj", blocks, gate_a_w)
                        .reshape(bsz, seqlen, B_WIDTH) + gate_a_b).astype(f32))
    i_g = jax.nn.sigmoid((jnp.einsum("btni,nij->btnj", blocks, gate_x_w)
                          .reshape(bsz, seqlen, B_WIDTH) + gate_x_b).astype(f32))
    log_a = -LRU_C * r * jax.nn.softplus(-lam.astype(f32))
    a = jnp.exp(log_a)
    mult = jnp.sqrt(-jnp.expm1(2.0 * log_a))
    u = mult * i_g * xc.astype(f32)
    _, h = lax.associative_scan(lru_combine, (a, u), axis=1)
    return h.astype(xb.dtype) * jax.nn.gelu(gb)


def even_mixer(h, w_in, mu, w0, w2, a0, a2, g2, k_k, k_a, r_k, lnx_g, lnx_b,
               conv_w, conv_b, gate_a_w, gate_a_b, gate_x_w, gate_x_b, lam, w_out):
    proj = h @ w_in
    pa, xb, gb = jnp.split(proj, (A_PROJ, A_PROJ + B_WIDTH), axis=-1)
    ya = rwkv7_time_mix(pa, mu, w0, w2, a0, a2, g2, k_k, k_a, r_k, lnx_g, lnx_b)
    yb = rglru_mix(xb, gb, conv_w, conv_b, gate_a_w, gate_a_b, gate_x_w, gate_x_b, lam)
    return jnp.concatenate([ya, yb], axis=-1) @ w_out


def stick_breaking_mixer(h, w_qkv, w_out):
    bsz, seqlen, _ = h.shape
    q, k, v = jnp.split(h @ w_qkv, 3, axis=-1)
    q = q.reshape(bsz, seqlen, C_HEADS, C_HEAD_DIM)
    k = k.reshape(bsz, seqlen, C_HEADS, C_HEAD_DIM)
    v = v.reshape(bsz, seqlen, C_HEADS, C_HEAD_DIM)
    scale = 1.0 / math.sqrt(C_HEAD_DIM)
    outs = []
    for s0 in range(0, seqlen, Q_BLOCK):
        end = s0 + Q_BLOCK
        qb, kb, vb = q[:, s0:end], k[:, :end], v[:, :end]
        z = jnp.einsum("bqhd,bkhd->bhqk", qb, kb).astype(jnp.float32) * scale
        q_pos = s0 + jnp.arange(Q_BLOCK)
        k_pos = jnp.arange(end)
        mask = k_pos[None, :] < q_pos[:, None]
        log_1m = jnp.where(mask, jax.nn.log_sigmoid(-z), 0.0)
        between = lax.cumsum(log_1m, axis=3, reverse=True) - log_1m
        att = jnp.where(mask, jnp.exp(jax.nn.log_sigmoid(z) + between), 0.0)
        outs.append(jnp.einsum("bhqk,bkhd->bqhd", att.astype(vb.dtype), vb))
    o = jnp.concatenate(outs, axis=1).reshape(bsz, seqlen, C_WIDTH)
    return o @ w_out


def macaron_layer(x, mixer, ffn1_pre_g, ffn1_post_g, ffn1_w_in, ffn1_w_out,
                  mix_pre_g, mix_post_g, ffn2_pre_g, ffn2_post_g, ffn2_w_in, ffn2_w_out):
    x = x + FFN_HALF * rmsnorm(swiglu(rmsnorm(x, ffn1_pre_g), ffn1_w_in, ffn1_w_out), ffn1_post_g)
    x = x + rmsnorm(mixer(rmsnorm(x, mix_pre_g)), mix_post_g)
    x = x + FFN_HALF * rmsnorm(swiglu(rmsnorm(x, ffn2_pre_g), ffn2_w_in, ffn2_w_out), ffn2_post_g)
    return x


def setup_inputs(seed: int = 0) -> dict:
    key = jax.random.key(seed)
    keys = iter(jax.random.split(key, 64))
    f32 = jnp.float32

    def nrm(shape, scale):
        return scale * jax.random.normal(next(keys), shape, f32)

    def gain(n):
        return 1.0 + nrm((n,), 0.05)

    def ffn_params(prefix, d):
        d[prefix + "_pre_g"] = gain(D_MODEL)
        d[prefix + "_post_g"] = gain(D_MODEL)
        d[prefix + "_w_in"] = nrm((D_MODEL, 2 * D_FF), D_MODEL ** -0.5)
        d[prefix + "_w_out"] = nrm((D_FF, D_MODEL), D_FF ** -0.5)

    d = {"x": nrm((BATCH, SEQ, D_MODEL), 1.0)}
    ffn_params("l0_ffn1", d)
    d["l0_mix_pre_g"] = gain(D_MODEL)
    d["l0_mix_post_g"] = gain(D_MODEL)
    d["l0_w_in"] = nrm((D_MODEL, IN_PROJ), D_MODEL ** -0.5)
    d["l0_mu"] = jax.random.uniform(next(keys), (A_PROJ,), f32, 0.0, 1.0)
    d["l0_w0"] = jnp.linspace(-6.0, -1.0, A_WIDTH, dtype=f32) + nrm((A_WIDTH,), 0.1)
    d["l0_w2"] = nrm((W_LORA, A_WIDTH), 0.1)
    d["l0_a0"] = nrm((A_WIDTH,), 0.1)
    d["l0_a2"] = nrm((A_LORA, A_WIDTH), A_LORA ** -0.5)
    d["l0_g2"] = nrm((G_LORA, A_WIDTH), G_LORA ** -0.5)
    d["l0_k_k"] = 0.85 + nrm((A_WIDTH,), 0.02)
    d["l0_k_a"] = 1.0 + nrm((A_WIDTH,), 0.02)
    d["l0_r_k"] = nrm((A_HEADS, A_HEAD_DIM), 0.1)
    d["l0_lnx_g"] = gain(A_WIDTH)
    d["l0_lnx_b"] = nrm((A_WIDTH,), 0.02)
    d["l0_conv_w"] = nrm((CONV_WIDTH, B_WIDTH), CONV_WIDTH ** -0.5)
    d["l0_conv_b"] = nrm((B_WIDTH,), 0.02)
    d["l0_gate_a_w"] = nrm((B_BLOCKS, B_BLOCK_DIM, B_BLOCK_DIM), B_BLOCK_DIM ** -0.5)
    d["l0_gate_a_b"] = nrm((B_WIDTH,), 0.02)
    d["l0_gate_x_w"] = nrm((B_BLOCKS, B_BLOCK_DIM, B_BLOCK_DIM), B_BLOCK_DIM ** -0.5)
    d["l0_gate_x_b"] = nrm((B_WIDTH,), 0.02)
    a_pow = jax.random.uniform(next(keys), (B_WIDTH,), f32, 0.9, 0.999) ** (1.0 / LRU_C)
    d["l0_lambda"] = jnp.log(a_pow) - jnp.log1p(-a_pow)
    d["l0_w_out"] = nrm((A_WIDTH + B_WIDTH, D_MODEL), (A_WIDTH + B_WIDTH) ** -0.5)
    ffn_params("l0_ffn2", d)
    ffn_params("l1_ffn1", d)
    d["l1_mix_pre_g"] = gain(D_MODEL)
    d["l1_mix_post_g"] = gain(D_MODEL)
    d["l1_w_qkv"] = nrm((D_MODEL, 3 * C_WIDTH), D_MODEL ** -0.5)
    d["l1_w_out"] = nrm((C_WIDTH, D_MODEL), C_WIDTH ** -0.5)
    ffn_params("l1_ffn2", d)
    return d


def reference(x,
              l0_ffn1_pre_g, l0_ffn1_post_g, l0_ffn1_w_in, l0_ffn1_w_out,
              l0_mix_pre_g, l0_mix_post_g, l0_w_in, l0_mu, l0_w0, l0_w2, l0_a0, l0_a2,
              l0_g2, l0_k_k, l0_k_a, l0_r_k, l0_lnx_g, l0_lnx_b, l0_conv_w, l0_conv_b,
              l0_gate_a_w, l0_gate_a_b, l0_gate_x_w, l0_gate_x_b, l0_lambda, l0_w_out,
              l0_ffn2_pre_g, l0_ffn2_post_g, l0_ffn2_w_in, l0_ffn2_w_out,
              l1_ffn1_pre_g, l1_ffn1_post_g, l1_ffn1_w_in, l1_ffn1_w_out,
              l1_mix_pre_g, l1_mix_post_g, l1_w_qkv, l1_w_out,
              l1_ffn2_pre_g, l1_ffn2_post_g, l1_ffn2_w_in, l1_ffn2_w_out):
    def layer0(t):
        mixer = lambda h: even_mixer(h, l0_w_in, l0_mu, l0_w0, l0_w2, l0_a0, l0_a2, l0_g2,
                                     l0_k_k, l0_k_a, l0_r_k, l0_lnx_g, l0_lnx_b,
                                     l0_conv_w, l0_conv_b, l0_gate_a_w, l0_gate_a_b,
                                     l0_gate_x_w, l0_gate_x_b, l0_lambda, l0_w_out)
        return macaron_layer(t, mixer, l0_ffn1_pre_g, l0_ffn1_post_g, l0_ffn1_w_in, l0_ffn1_w_out,
                             l0_mix_pre_g, l0_mix_post_g,
                             l0_ffn2_pre_g, l0_ffn2_post_g, l0_ffn2_w_in, l0_ffn2_w_out)

    def layer1(t):
        mixer = lambda h: stick_breaking_mixer(h, l1_w_qkv, l1_w_out)
        return macaron_layer(t, mixer, l1_ffn1_pre_g, l1_ffn1_post_g, l1_ffn1_w_in, l1_ffn1_w_out,
                             l1_mix_pre_g, l1_mix_post_g,
                             l1_ffn2_pre_g, l1_ffn2_post_g, l1_ffn2_w_in, l1_ffn2_w_out)

    layers = [layer0, layer1]
    for i in range(DEPTH):
        x = layers[i](x)
    return x
```

```python
import functools
import math

import jax
import jax.numpy as jnp
from jax import lax
from jax.experimental import pallas as pl
from jax.experimental.pallas import tpu as pltpu

F32 = jnp.float32
BF16 = jnp.bfloat16

NORM_EPS = 1e-6
GN_EPS = 64e-5
LRU_C = 8.0
HEAD = 64
LANES = 128
CHUNK = 64
CONV_WIDTH = 4
VMEM_LIMIT = 56 * 1024 * 1024


def _params(n_axes, vmem=VMEM_LIMIT):
    return pltpu.CompilerParams(
        dimension_semantics=("arbitrary",) * n_axes, vmem_limit_bytes=vmem)


def _bf(x):
    return x.astype(BF16)


def _dot(a, b):
    return jnp.dot(a, b, preferred_element_type=F32)


def _dot_nt(a, b):
    return lax.dot_general(a, b, (((1,), (1,)), ((), ())), preferred_element_type=F32)


def _dot_hilo(x, e):
    hi = _bf(x)
    lo = _bf(x - hi.astype(F32))
    return _dot(hi, e) + _dot(lo, e)


def _rms(x, g):
    return x * lax.rsqrt(jnp.mean(x * x, axis=-1, keepdims=True) + NORM_EPS) * g


def _sigmoid(x):
    return 1.0 / (1.0 + jnp.exp(-x))


def _softplus(x):
    return jnp.maximum(x, 0.0) + jnp.log1p(jnp.exp(-jnp.abs(x)))


def _gelu_tanh(x):
    c = math.sqrt(2.0 / math.pi)
    return 0.5 * x * (1.0 + jnp.tanh(c * (x + 0.044715 * (x * x * x))))


def _row_spec(tm, n):
    return pl.BlockSpec((tm, n), lambda i, *_: (i, 0))


def _const_spec(shape):
    return pl.BlockSpec(shape, lambda *_: (0,) * len(shape))


def _ffn_body(x_ref, gpre_ref, gpost_ref, wg_ref, wu_ref, wo_ref, o_ref, h_scr, acc_scr):
    j = pl.program_id(1)

    @pl.when(j == 0)
    def _():
        h_scr[...] = _bf(_rms(x_ref[...], gpre_ref[...]))
        acc_scr[...] = jnp.zeros_like(acc_scr)

    h = h_scr[...]
    gate = _dot(h, wg_ref[...])
    up = _dot(h, wu_ref[...])
    act = _bf(gate * _sigmoid(gate) * up)
    acc_scr[...] += _dot(act, wo_ref[...])

    @pl.when(j == pl.num_programs(1) - 1)
    def _():
        o_ref[...] = x_ref[...] + 0.5 * _rms(acc_scr[...], gpost_ref[...])


def _ffn(x, g_pre, g_post, w_in, w_out, *, tm=1024, tf=256):
    m, d = x.shape
    tm = min(tm, m)
    f = w_out.shape[0]
    nf = f // tf
    w_in = _bf(w_in)
    return pl.pallas_call(
        _ffn_body,
        grid=(m // tm, nf),
        in_specs=[
            pl.BlockSpec((tm, d), lambda i, j: (i, 0)),
            _const_spec((1, d)),
            _const_spec((1, d)),
            pl.BlockSpec((d, tf), lambda i, j: (0, j)),
            pl.BlockSpec((d, tf), lambda i, j: (0, j + nf)),
            pl.BlockSpec((tf, d), lambda i, j: (j, 0)),
        ],
        out_specs=pl.BlockSpec((tm, d), lambda i, j: (i, 0)),
        out_shape=jax.ShapeDtypeStruct((m, d), F32),
        scratch_shapes=[pltpu.VMEM((tm, d), BF16), pltpu.VMEM((tm, d), F32)],
        compiler_params=_params(2),
        name="ffn",
    )(x, g_pre.reshape(1, d), g_post.reshape(1, d), w_in, w_in, _bf(w_out))


def _proj_body(x_ref, g_ref, w_ref, *o_refs):
    h = _bf(_rms(x_ref[...], g_ref[...]))
    y = _dot(h, w_ref[...])
    off = 0
    for o_ref in o_refs:
        n = o_ref.shape[-1]
        o_ref[...] = y[:, off:off + n].astype(o_ref.dtype)
        off += n


def _proj(x, g, w, splits, dtype, *, tm=512):
    m, d = x.shape
    tm = min(tm, m)
    n = w.shape[1]
    assert sum(splits) == n
    return pl.pallas_call(
        _proj_body,
        grid=(m // tm,),
        in_specs=[_row_spec(tm, d), _const_spec((1, d)), _const_spec((d, n))],
        out_specs=[_row_spec(tm, s) for s in splits],
        out_shape=[jax.ShapeDtypeStruct((m, s), dtype) for s in splits],
        compiler_params=_params(1),
        name="proj",
    )(x, g.reshape(1, d), _bf(w))


def _outproj_body(*refs, n_in):
    a_refs = refs[:n_in]
    w_refs = refs[n_in:2 * n_in]
    res_ref, g_ref, o_ref = refs[2 * n_in:]
    y = _dot(a_refs[0][...], w_refs[0][...])
    for a_ref, w_ref in zip(a_refs[1:], w_refs[1:]):
        y += _dot(a_ref[...], w_ref[...])
    o_ref[...] = res_ref[...] + _rms(y, g_ref[...])


def _outproj(acts, ws, res, g, *, tm=512):
    m, d = res.shape
    tm = min(tm, m)
    n_in = len(acts)
    return pl.pallas_call(
        functools.partial(_outproj_body, n_in=n_in),
        grid=(m // tm,),
        in_specs=([_row_spec(tm, a.shape[1]) for a in acts]
                  + [_const_spec(w.shape) for w in ws]
                  + [_row_spec(tm, d), _const_spec((1, d))]),
        out_specs=_row_spec(tm, d),
        out_shape=jax.ShapeDtypeStruct((m, d), F32),
        compiler_params=_params(1),
        name="outproj",
    )(*acts, *[_bf(w) for w in ws], res, g.reshape(1, d))


def _solve_unit_lower(n, rhs):
    x = rhs
    p = n
    steps = int(math.log2(CHUNK))
    for i in range(steps):
        x = x + _dot(_bf(p), _bf(x))
        if i + 1 < steps:
            pb = _bf(p)
            p = _dot(pb, pb)
    return x


def _rwkv_body(pa_ref, mu_ref, w0_ref, a0_ref, kk_ref, ka_ref, rk_ref, lng_ref, lnb_ref,
               wwa_ref, g2_ref, tri_ref, hsum_ref, o_ref, prev_scr, s_scr):
    c = pl.program_id(1)
    width = o_ref.shape[-1]
    n_pairs = width // LANES

    @pl.when(c == 0)
    def _():
        prev_scr[...] = jnp.zeros_like(prev_scr)
        s_scr[...] = jnp.zeros_like(s_scr)

    p = pa_ref[0]
    row = lax.broadcasted_iota(jnp.int32, p.shape, 0)
    shifted = jnp.where(row == 0, prev_scr[...], pltpu.roll(p, 1, axis=0))
    prev_scr[...] = p[CHUNK - 1:CHUNK, :]
    xm = p + mu_ref[...] * (shifted - p)

    r = xm[:, 0:width]
    k = xm[:, width:2 * width]
    v = xm[:, 2 * width:3 * width]
    wa_lo = xm[:, 3 * width:3 * width + LANES]
    g_lo = xm[:, 3 * width + LANES:]
    lane_wa = lax.broadcasted_iota(jnp.int32, wa_lo.shape, 1)
    wa_act = jnp.where(lane_wa < LANES // 2, jnp.tanh(wa_lo), wa_lo)
    lora = _dot(_bf(wa_act), wwa_ref[...])
    w_log = -_softplus(-(w0_ref[...] + lora[:, :width])) - 0.5
    log_decay = -jnp.exp(w_log)
    a = _sigmoid(a0_ref[...] + lora[:, width:])
    gate = _dot(_bf(_sigmoid(g_lo)), g2_ref[...])

    hsum = hsum_ref[...]
    kk = k * kk_ref[...]
    kk = kk / jnp.maximum(jnp.sqrt(_dot_hilo(kk * kk, hsum)), 1e-12)
    k_h = k * (1.0 + (a - 1.0) * ka_ref[...])

    ld_hi = _bf(log_decay)
    ld_lo = _bf(log_decay - ld_hi.astype(F32))
    tri = tri_ref[...]
    cum = _dot(tri, ld_hi) + _dot(tri, ld_lo)
    dec_in = jnp.exp(cum)
    dec_out = jnp.exp(-cum)
    r_t = _bf(r * dec_in)
    a_t = _bf(-kk * jnp.exp(cum - log_decay))
    b_t = _bf(kk * a * dec_out)
    k_t = _bf(k_h * dec_out)
    w_last = dec_in[CHUNK - 1:CHUNK, :]
    v_b = _bf(v)

    lane = lax.broadcasted_iota(jnp.int32, (CHUNK, LANES), 1)
    trow = lax.broadcasted_iota(jnp.int32, (CHUNK, LANES), 0)
    src = jnp.where(lane < HEAD, lane, lane - HEAD)
    strict = src < trow
    incl = src <= trow
    first = lane < HEAD
    sr = lax.broadcasted_iota(jnp.int32, (LANES, LANES), 0)
    sc = lax.broadcasted_iota(jnp.int32, (LANES, LANES), 1)
    same_head = (sr < HEAD) == (sc < HEAD)
    zeros_blk = jnp.zeros((CHUNK, LANES), BF16)

    ys = []
    for pr in range(n_pairs):
        sl = slice(pr * LANES, (pr + 1) * LANES)
        at, rt, bt, kt, vt = a_t[:, sl], r_t[:, sl], b_t[:, sl], k_t[:, sl], v_b[:, sl]
        s0 = s_scr[pr]
        s0b = _bf(s0)
        bk = jnp.concatenate([bt, kt], axis=0)
        zv = jnp.concatenate([zeros_blk, vt], axis=0)
        a_s = _dot_nt(at, s0b)
        r_s = _dot_nt(rt, s0b)
        us, hs = [], []
        for e in range(2):
            sel = first if e == 0 else jnp.logical_not(first)
            g_e = jnp.where(strict, _dot_nt(jnp.where(sel, at, 0), bk), 0.0)
            h_e = jnp.where(incl, _dot_nt(jnp.where(sel, rt, 0), bk), 0.0)
            rhs = a_s + _dot(_bf(g_e), zv)
            us.append(_solve_unit_lower(g_e[:, :CHUNK], rhs))
            hs.append(_bf(h_e))
        u = jnp.where(first, us[0], us[1])
        uv = jnp.concatenate([_bf(u), vt], axis=0)
        ys.append(r_s + jnp.where(first, _dot(hs[0], uv), _dot(hs[1], uv)))
        upd = _dot(_bf(uv.astype(F32).T), bk)
        s_scr[pr] = jnp.where(same_head, (s0 + upd) * w_last[:, sl], 0.0)
    y = jnp.concatenate(ys, axis=-1)

    inv_n = 1.0 / HEAD
    mean = _dot_hilo(y, hsum) * inv_n
    dev = y - mean
    var = _dot_hilo(dev * dev, hsum) * inv_n
    yn = dev * lax.rsqrt(var + GN_EPS) * lng_ref[...] + lnb_ref[...]
    bonus = _dot_hilo(r * k_h * rk_ref[...], hsum) * v
    o_ref[0] = ((yn + bonus) * gate).astype(o_ref.dtype)


def _rwkv(pa, mu, w0, w2, a0, a2, g2, k_k, k_a, r_k, lnx_g, lnx_b):
    b, t, ap = pa.shape
    width = w0.shape[0]
    w_lora, a_lora = w2.shape[0], a2.shape[0]
    assert w_lora == HEAD and a_lora == HEAD and width % LANES == 0
    n_pairs = width // LANES
    wwa = jnp.zeros((w_lora + a_lora, 2 * width), F32)
    wwa = wwa.at[:w_lora, :width].set(w2).at[w_lora:, width:].set(a2)
    ti = jnp.arange(CHUNK)
    tri = (ti[None, :] <= ti[:, None]).astype(BF16)
    hi = jnp.arange(width) // HEAD
    hsum = (hi[:, None] == hi[None, :]).astype(BF16)
    vec = lambda z: z.reshape(1, width)
    return pl.pallas_call(
        _rwkv_body,
        grid=(b, t // CHUNK),
        in_specs=[
            pl.BlockSpec((1, CHUNK, ap), lambda i, c: (i, c, 0)),
            _const_spec((1, ap)),
            _const_spec((1, width)), _const_spec((1, width)), _const_spec((1, width)),
            _const_spec((1, width)), _const_spec((1, width)), _const_spec((1, width)),
            _const_spec((1, width)),
            _const_spec((w_lora + a_lora, 2 * width)),
            _const_spec(g2.shape),
            _const_spec((CHUNK, CHUNK)),
            _const_spec((width, width)),
        ],
        out_specs=pl.BlockSpec((1, CHUNK, width), lambda i, c: (i, c, 0)),
        out_shape=jax.ShapeDtypeStruct((b, t, width), BF16),
        scratch_shapes=[pltpu.VMEM((1, ap), F32), pltpu.VMEM((n_pairs, LANES, LANES), F32)],
        compiler_params=_params(2),
        name="rwkv7",
    )(pa, mu.reshape(1, ap), vec(w0), vec(a0), vec(k_k), vec(k_a), vec(r_k), vec(lnx_g),
      vec(lnx_b), _bf(wwa), _bf(g2), tri, hsum)


def _rglru_body(xb_ref, gb_ref, cw_ref, cb_ref, wg_ref, bg_ref, lam_ref, o_ref,
                prev_scr, h_scr):
    i = pl.program_id(1)
    tt, width = o_ref.shape[1], o_ref.shape[2]

    @pl.when(i == 0)
    def _():
        prev_scr[...] = jnp.zeros_like(prev_scr)
        h_scr[...] = jnp.zeros_like(h_scr)

    x = xb_ref[0]
    prev = prev_scr[...]
    prev_scr[...] = x[tt - 8:, :]
    row8 = lax.broadcasted_iota(jnp.int32, (8, width), 0)

    def delayed(d):
        rolled = pltpu.roll(x, d, axis=0)
        head = jnp.where(row8 < d, pltpu.roll(prev, d, axis=0), rolled[:8])
        return jnp.concatenate([head, rolled[8:]], axis=0)

    xc = cb_ref[...] + cw_ref[CONV_WIDTH - 1:CONV_WIDTH, :] * x
    for d in range(1, CONV_WIDTH):
        xc = xc + cw_ref[CONV_WIDTH - 1 - d:CONV_WIDTH - d, :] * delayed(d)

    gates = _dot(_bf(xc), wg_ref[...]) + bg_ref[...]
    r_g = _sigmoid(gates[:, :width])
    i_g = _sigmoid(gates[:, width:])
    log_a = -LRU_C * r_g * _softplus(-lam_ref[...])
    a = jnp.exp(log_a)
    mult = jnp.sqrt(jnp.tanh(-log_a) * (1.0 + a * a))
    u = mult * i_g * xc

    row = lax.broadcasted_iota(jnp.int32, (tt, width), 0)
    d = 1
    while d < tt:
        keep = row >= d
        a_sh = jnp.where(keep, pltpu.roll(a, d, axis=0), 1.0)
        u_sh = jnp.where(keep, pltpu.roll(u, d, axis=0), 0.0)
        u = a * u_sh + u
        a = a * a_sh
        d *= 2
    h = u + a * h_scr[...]
    h_scr[...] = h[tt - 1:, :]
    o_ref[0] = (h * _gelu_tanh(gb_ref[0])).astype(o_ref.dtype)


def _rglru(xb, gb, conv_w, conv_b, gate_a_w, gate_a_b, gate_x_w, gate_x_b, lam, *, tt=256):
    b, t, width = xb.shape
    tt = min(tt, t)
    nb, bd, _ = gate_a_w.shape
    eye = jnp.eye(nb, dtype=F32)
    dense = lambda w: jnp.einsum("nij,nm->nimj", w, eye).reshape(width, width)
    wg = _bf(jnp.concatenate([dense(gate_a_w), dense(gate_x_w)], axis=1))
    bg = jnp.concatenate([gate_a_b, gate_x_b]).reshape(1, 2 * width)
    tile = pl.BlockSpec((1, tt, width), lambda i, j: (i, j, 0))
    return pl.pallas_call(
        _rglru_body,
        grid=(b, t // tt),
        in_specs=[tile, tile, _const_spec((CONV_WIDTH, width)), _const_spec((1, width)),
                  _const_spec((width, 2 * width)), _const_spec((1, 2 * width)),
                  _const_spec((1, width))],
        out_specs=tile,
        out_shape=jax.ShapeDtypeStruct((b, t, width), BF16),
        scratch_shapes=[pltpu.VMEM((8, width), F32), pltpu.VMEM((1, width), F32)],
        compiler_params=_params(2),
        name="rglru",
    )(xb, gb, conv_w, conv_b.reshape(1, width), wg, bg, lam.reshape(1, width))


def _sb_body(q_ref, k_ref, v_ref, o_ref, *, blk):
    qi = pl.program_id(2)
    q = q_ref[0] * (1.0 / math.sqrt(HEAD))
    lane = lax.broadcasted_iota(jnp.int32, (blk, LANES), 1)
    first = lane < HEAD
    qs = (jnp.where(first, q, 0), jnp.where(first, 0, q))
    kr = lax.broadcasted_iota(jnp.int32, (blk, blk), 0)
    kc = lax.broadcasted_iota(jnp.int32, (blk, blk), 1)
    later = (kr > kc).astype(BF16)
    causal = kc < kr

    def block(j, carry, masked):
        start = pl.multiple_of(j * blk, blk)
        k = k_ref[0, pl.ds(start, blk), :]
        v = v_ref[0, pl.ds(start, blk), :]
        out = []
        for e in range(2):
            acc, run = carry[e]
            z = _dot_nt(qs[e], k)
            neg = -_softplus(z)
            if masked:
                neg = jnp.where(causal, neg, 0.0)
            logit = z + neg + _dot_hilo(neg, later) + run
            att = jnp.exp(logit)
            if masked:
                att = jnp.where(causal, att, 0.0)
            acc = acc + _dot(_bf(att), v)
            run = run + jnp.sum(neg, axis=-1, keepdims=True)
            out.append((acc, run))
        return tuple(out)

    zero = (jnp.zeros((blk, LANES), F32), jnp.zeros((blk, 1), F32))
    carry = block(qi, (zero, zero), True)
    carry = lax.fori_loop(0, qi, lambda jj, cr: block(qi - 1 - jj, cr, False), carry)
    o_ref[0] = jnp.where(first, carry[0][0], carry[1][0]).astype(o_ref.dtype)


def _stick_breaking(q, k, v, *, blk=128):
    b, t, width = q.shape
    n_pairs = width // LANES
    return pl.pallas_call(
        functools.partial(_sb_body, blk=blk),
        grid=(b, n_pairs, t // blk),
        in_specs=[
            pl.BlockSpec((1, blk, LANES), lambda i, p, j: (i, j, p)),
            pl.BlockSpec((1, t, LANES), lambda i, p, j: (i, 0, p)),
            pl.BlockSpec((1, t, LANES), lambda i, p, j: (i, 0, p)),
        ],
        out_specs=pl.BlockSpec((1, blk, LANES), lambda i, p, j: (i, j, p)),
        out_shape=jax.ShapeDtypeStruct((b, t, width), BF16),
        compiler_params=_params(3),
        name="stick_breaking",
    )(q, k, v)


def kernel(x, l0_ffn1_pre_g, l0_ffn1_post_g, l0_ffn1_w_in, l0_ffn1_w_out, l0_mix_pre_g, l0_mix_post_g, l0_w_in, l0_mu, l0_w0, l0_w2, l0_a0, l0_a2, l0_g2, l0_k_k, l0_k_a, l0_r_k, l0_lnx_g, l0_lnx_b, l0_conv_w, l0_conv_b, l0_gate_a_w, l0_gate_a_b, l0_gate_x_w, l0_gate_x_b, l0_lambda, l0_w_out, l0_ffn2_pre_g, l0_ffn2_post_g, l0_ffn2_w_in, l0_ffn2_w_out, l1_ffn1_pre_g, l1_ffn1_post_g, l1_ffn1_w_in, l1_ffn1_w_out, l1_mix_pre_g, l1_mix_post_g, l1_w_qkv, l1_w_out, l1_ffn2_pre_g, l1_ffn2_post_g, l1_ffn2_w_in, l1_ffn2_w_out):
    b, t, d = x.shape
    a_width = l0_w0.shape[0]
    b_width = l0_lambda.shape[0]
    a_proj = l0_mu.shape[0]
    h = x.reshape(b * t, d)

    h = _ffn(h, l0_ffn1_pre_g, l0_ffn1_post_g, l0_ffn1_w_in, l0_ffn1_w_out)
    pa, xb, gb = _proj(h, l0_mix_pre_g, l0_w_in, (a_proj, b_width, b_width), F32)
    ya = _rwkv(pa.reshape(b, t, a_proj), l0_mu, l0_w0, l0_w2, l0_a0, l0_a2, l0_g2, l0_k_k,
               l0_k_a, l0_r_k.reshape(-1), l0_lnx_g, l0_lnx_b)
    yb = _rglru(xb.reshape(b, t, b_width), gb.reshape(b, t, b_width), l0_conv_w, l0_conv_b,
                l0_gate_a_w, l0_gate_a_b, l0_gate_x_w, l0_gate_x_b, l0_lambda)
    h = _outproj([ya.reshape(b * t, a_width), yb.reshape(b * t, b_width)],
                 [l0_w_out[:a_width], l0_w_out[a_width:]], h, l0_mix_post_g)
    h = _ffn(h, l0_ffn2_pre_g, l0_ffn2_post_g, l0_ffn2_w_in, l0_ffn2_w_out)

    h = _ffn(h, l1_ffn1_pre_g, l1_ffn1_post_g, l1_ffn1_w_in, l1_ffn1_w_out)
    q, k, v = _proj(h, l1_mix_pre_g, l1_w_qkv, (d, d, d), BF16)
    o = _stick_breaking(q.reshape(b, t, d), k.reshape(b, t, d), v.reshape(b, t, d))
    h = _outproj([o.reshape(b * t, d)], [l1_w_out], h, l1_mix_post_g)
    h = _ffn(h, l1_ffn2_pre_g, l1_ffn2_post_g, l1_ffn2_w_in, l1_ffn2_w_out)
    return h.reshape(b, t, d)
```

```python
import functools
import math

import jax
import jax.numpy as jnp
from jax import lax
from jax.experimental import pallas as pl
from jax.experimental.pallas import tpu as pltpu

F32 = jnp.float32
BF16 = jnp.bfloat16

NORM_EPS = 1e-6
GN_EPS = 64e-5
LRU_C = 8.0
HEAD = 64
LANES = 128
CHUNK = 64
CONV_WIDTH = 4
VMEM_LIMIT = 56 * 1024 * 1024


def _params(n_axes, vmem=VMEM_LIMIT):
    return pltpu.CompilerParams(
        dimension_semantics=("arbitrary",) * n_axes, vmem_limit_bytes=vmem)


def _bf(x):
    return x.astype(BF16)


def _dot(a, b):
    return jnp.dot(a, b, preferred_element_type=F32)


def _dot_nt(a, b):
    return lax.dot_general(a, b, (((1,), (1,)), ((), ())), preferred_element_type=F32)


def _dot_hilo(x, e):
    hi = _bf(x)
    lo = _bf(x - hi.astype(F32))
    return _dot(hi, e) + _dot(lo, e)


def _rms(x, g):
    return x * lax.rsqrt(jnp.mean(x * x, axis=-1, keepdims=True) + NORM_EPS) * g


def _sigmoid(x):
    return 1.0 / (1.0 + jnp.exp(-x))


def _softplus(x):
    return jnp.maximum(x, 0.0) + jnp.log1p(jnp.exp(-jnp.abs(x)))


def _gelu_tanh(x):
    c = math.sqrt(2.0 / math.pi)
    return 0.5 * x * (1.0 + jnp.tanh(c * (x + 0.044715 * (x * x * x))))


def _row_spec(tm, n):
    return pl.BlockSpec((tm, n), lambda i, *_: (i, 0))


def _const_spec(shape):
    return pl.BlockSpec(shape, lambda *_: (0,) * len(shape))


def _ffn_body(x_ref, gpre_ref, gpost_ref, wg_ref, wu_ref, wo_ref, o_ref, h_scr, acc_scr):
    j = pl.program_id(1)

    @pl.when(j == 0)
    def _():
        h_scr[...] = _bf(_rms(x_ref[...], gpre_ref[...]))
        acc_scr[...] = jnp.zeros_like(acc_scr)

    h = h_scr[...]
    gate = _dot(h, wg_ref[...])
    up = _dot(h, wu_ref[...])
    act = _bf(gate * _sigmoid(gate) * up)
    acc_scr[...] += _dot(act, wo_ref[...])

    @pl.when(j == pl.num_programs(1) - 1)
    def _():
        o_ref[...] = x_ref[...] + 0.5 * _rms(acc_scr[...], gpost_ref[...])


def _ffn(x, g_pre, g_post, w_in, w_out, *, tm=1024, tf=256):
    m, d = x.shape
    tm = min(tm, m)
    f = w_out.shape[0]
    nf = f // tf
    w_in = _bf(w_in)
    return pl.pallas_call(
        _ffn_body,
        grid=(m // tm, nf),
        in_specs=[
            pl.BlockSpec((tm, d), lambda i, j: (i, 0)),
            _const_spec((1, d)),
            _const_spec((1, d)),
            pl.BlockSpec((d, tf), lambda i, j: (0, j)),
            pl.BlockSpec((d, tf), lambda i, j: (0, j + nf)),
            pl.BlockSpec((tf, d), lambda i, j: (j, 0)),
        ],
        out_specs=pl.BlockSpec((tm, d), lambda i, j: (i, 0)),
        out_shape=jax.ShapeDtypeStruct((m, d), F32),
        scratch_shapes=[pltpu.VMEM((tm, d), BF16), pltpu.VMEM((tm, d), F32)],
        compiler_params=_params(2),
        name="ffn",
    )(x, g_pre.reshape(1, d), g_post.reshape(1, d), w_in, w_in, _bf(w_out))


def _proj_body(x_ref, g_ref, w_ref, *o_refs):
    h = _bf(_rms(x_ref[...], g_ref[...]))
    y = _dot(h, w_ref[...])
    off = 0
    for o_ref in o_refs:
        n = o_ref.shape[-1]
        o_ref[...] = y[:, off:off + n].astype(o_ref.dtype)
        off += n


def _proj(x, g, w, splits, dtype, *, tm=512):
    m, d = x.shape
    tm = min(tm, m)
    n = w.shape[1]
    assert sum(splits) == n
    return pl.pallas_call(
        _proj_body,
        grid=(m // tm,),
        in_specs=[_row_spec(tm, d), _const_spec((1, d)), _const_spec((d, n))],
        out_specs=[_row_spec(tm, s) for s in splits],
        out_shape=[jax.ShapeDtypeStruct((m, s), dtype) for s in splits],
        compiler_params=_params(1),
        name="proj",
    )(x, g.reshape(1, d), _bf(w))


def _outproj_body(*refs, n_in):
    a_refs = refs[:n_in]
    w_refs = refs[n_in:2 * n_in]
    res_ref, g_ref, o_ref = refs[2 * n_in:]
    y = _dot(a_refs[0][...], w_refs[0][...])
    for a_ref, w_ref in zip(a_refs[1:], w_refs[1:]):
        y += _dot(a_ref[...], w_ref[...])
    o_ref[...] = res_ref[...] + _rms(y, g_ref[...])


def _outproj(acts, ws, res, g, *, tm=512):
    m, d = res.shape
    tm = min(tm, m)
    n_in = len(acts)
    return pl.pallas_call(
        functools.partial(_outproj_body, n_in=n_in),
        grid=(m // tm,),
        in_specs=([_row_spec(tm, a.shape[1]) for a in acts]
                  + [_const_spec(w.shape) for w in ws]
                  + [_row_spec(tm, d), _const_spec((1, d))]),
        out_specs=_row_spec(tm, d),
        out_shape=jax.ShapeDtypeStruct((m, d), F32),
        compiler_params=_params(1),
        name="outproj",
    )(*acts, *[_bf(w) for w in ws], res, g.reshape(1, d))


def _solve_unit_lower(ns, xs):
    ps = [_bf(n) for n in ns]
    steps = int(math.log2(CHUNK))
    for i in range(steps):
        xs = [x + _dot(p, _bf(x)) for p, x in zip(ps, xs)]
        if i + 1 < steps:
            ps = [_bf(_dot(p, p)) for p in ps]
    return xs


def _rwkv_body(pa_ref, mu_ref, w0_ref, a0_ref, kk_ref, ka_ref, rk_ref, lng_ref, lnb_ref,
               wwa_ref, g2_ref, tri_ref, hsum_ref, o_ref, prev_scr, s_scr):
    step = pl.program_id(1)
    rows, width = o_ref.shape[1], o_ref.shape[2]
    n_pairs = width // LANES
    n_chunks = rows // CHUNK

    @pl.when(step == 0)
    def _():
        prev_scr[...] = jnp.zeros_like(prev_scr)
        s_scr[...] = jnp.zeros_like(s_scr)

    p = pa_ref[0]
    row = lax.broadcasted_iota(jnp.int32, p.shape, 0)
    shifted = jnp.where(row == 0, prev_scr[...], pltpu.roll(p, 1, axis=0))
    prev_scr[...] = p[rows - 1:rows, :]
    xm = p + mu_ref[...] * (shifted - p)

    r = xm[:, 0:width]
    k = xm[:, width:2 * width]
    v = xm[:, 2 * width:3 * width]
    wa_lo = xm[:, 3 * width:3 * width + LANES]
    g_lo = xm[:, 3 * width + LANES:]
    lane_wa = lax.broadcasted_iota(jnp.int32, wa_lo.shape, 1)
    wa_act = jnp.where(lane_wa < LANES // 2, jnp.tanh(wa_lo), wa_lo)
    lora = _dot(_bf(wa_act), wwa_ref[...])
    w_log = -_softplus(-(w0_ref[...] + lora[:, :width])) - 0.5
    log_decay = -jnp.exp(w_log)
    a = _sigmoid(a0_ref[...] + lora[:, width:])
    gate = _dot(_bf(_sigmoid(g_lo)), g2_ref[...])

    hsum = hsum_ref[...]
    kk = k * kk_ref[...]
    kk = kk / jnp.maximum(jnp.sqrt(_dot_hilo(kk * kk, hsum)), 1e-12)
    k_h = k * (1.0 + (a - 1.0) * ka_ref[...])

    ld_hi = _bf(log_decay)
    ld_lo = _bf(log_decay - ld_hi.astype(F32))
    tri = tri_ref[...]
    cum = _dot(tri, ld_hi) + _dot(tri, ld_lo)
    dec_in = jnp.exp(cum)
    dec_out = jnp.exp(-cum)
    r_f = r * dec_in
    a_f = -kk * jnp.exp(cum - log_decay)
    b_t = _bf(kk * a * dec_out)
    k_t = _bf(k_h * dec_out)

    lane = lax.broadcasted_iota(jnp.int32, (CHUNK, LANES), 1)
    trow = lax.broadcasted_iota(jnp.int32, (CHUNK, LANES), 0)
    first = lane < HEAD
    src = jnp.where(first, lane, lane - HEAD)
    strict = src < trow
    incl = src <= trow
    lane2 = lax.broadcasted_iota(jnp.int32, (CHUNK, 2 * LANES), 1)
    first2 = jnp.where(lane2 < LANES, lane2, lane2 - LANES) < HEAD
    sr = lax.broadcasted_iota(jnp.int32, (LANES, LANES), 0)
    sc = lax.broadcasted_iota(jnp.int32, (LANES, LANES), 1)
    same_head = (sr < HEAD) == (sc < HEAD)
    z16 = jnp.zeros((CHUNK, LANES), BF16)
    z32 = jnp.zeros((CHUNK, LANES), F32)

    inst = [(c, pr) for c in range(n_chunks) for pr in range(n_pairs)]
    blk = lambda z, c, pr: z[c * CHUNK:(c + 1) * CHUNK, pr * LANES:(pr + 1) * LANES]
    afs = [blk(a_f, c, pr) for c, pr in inst]
    rfs = [blk(r_f, c, pr) for c, pr in inst]
    vfs = [blk(v, c, pr) for c, pr in inst]
    vts = [_bf(vf) for vf in vfs]
    bks = [jnp.concatenate([blk(b_t, c, pr), blk(k_t, c, pr)], axis=0) for c, pr in inst]
    kbs = [jnp.concatenate([blk(k_t, c, pr), blk(b_t, c, pr)], axis=0) for c, pr in inst]
    gh0s = [_dot_nt(jnp.concatenate([jnp.where(first, _bf(af), 0), jnp.where(first, _bf(rf), 0)],
                                    axis=0), bk) for af, rf, bk in zip(afs, rfs, bks)]
    gh1s = [_dot_nt(jnp.concatenate([jnp.where(first, 0, _bf(af)), jnp.where(first, 0, _bf(rf))],
                                    axis=0), kb) for af, rf, kb in zip(afs, rfs, kbs)]
    g0s = [jnp.where(strict, gh[:CHUNK], 0.0) for gh in gh0s]
    g1s = [jnp.where(strict, gh[:CHUNK], 0.0) for gh in gh1s]
    lhss = [jnp.concatenate(
        [jnp.concatenate([_bf(jnp.where(incl, ga[CHUNK:], 0.0)), z16], axis=1),
         jnp.concatenate([z16, _bf(jnp.where(incl, gb[CHUNK:], 0.0))], axis=1)], axis=0)
        for ga, gb in zip(gh0s, gh1s)]
    n_blks = [jnp.concatenate([jnp.where(first, g0, 0.0), jnp.where(first, 0.0, g1)], axis=0)
              for g0, g1 in zip(g0s, g1s)]
    akvs = [_dot(_bf(jnp.concatenate([jnp.where(first, 0.0, g0), jnp.where(first, g1, 0.0)],
                                     axis=0)), jnp.concatenate([vt, vt], axis=0))
            for g0, g1, vt in zip(g0s, g1s, vts)]
    xs = [jnp.concatenate([jnp.concatenate([af, af], axis=0), akv], axis=1)
          for af, akv in zip(afs, akvs)]
    xs = _solve_unit_lower(n_blks, xs)
    lows = [jnp.concatenate([jnp.where(first2, x[:CHUNK], x[CHUNK:]),
                             jnp.concatenate([z32, vf], axis=1)], axis=0)
            for x, vf in zip(xs, vfs)]
    lowbs = [_bf(low) for low in lows]
    outs = [_dot(lhs, jnp.concatenate([lowb, lowb[CHUNK:], lowb[:CHUNK]], axis=0))
            for lhs, lowb in zip(lhss, lowbs)]
    sels = [jnp.where(first2, out[:CHUNK], out[CHUNK:]) for out in outs]
    mcs = [_dot(_bf(low.T), bk) for low, bk in zip(lows, bks)]

    states = [s_scr[pr] for pr in range(n_pairs)]
    y_rows = []
    for c in range(n_chunks):
        y_pairs = []
        for pr in range(n_pairs):
            i = c * n_pairs + pr
            w_last = dec_in[(c + 1) * CHUNK - 1:(c + 1) * CHUNK, pr * LANES:(pr + 1) * LANES]
            g_tilde = rfs[i] + sels[i][:, :LANES]
            y_intra = sels[i][:, LANES:]
            m_lr = jnp.where(same_head, mcs[i][:LANES], 0.0) * w_last
            c_mat = jnp.where(same_head, mcs[i][LANES:], 0.0) * w_last
            s = states[pr]
            sb = _bf(s)
            y_pairs.append(_dot_nt(_bf(g_tilde), sb) + y_intra)
            states[pr] = s * w_last + _dot(sb, _bf(m_lr)) + c_mat
        y_rows.append(jnp.concatenate(y_pairs, axis=-1))
    for pr in range(n_pairs):
        s_scr[pr] = states[pr]
    y = jnp.concatenate(y_rows, axis=0)

    inv_n = 1.0 / HEAD
    mean = _dot_hilo(y, hsum) * inv_n
    dev = y - mean
    var = _dot_hilo(dev * dev, hsum) * inv_n
    yn = dev * lax.rsqrt(var + GN_EPS) * lng_ref[...] + lnb_ref[...]
    bonus = _dot_hilo(r * k_h * rk_ref[...], hsum) * v
    o_ref[0] = ((yn + bonus) * gate).astype(o_ref.dtype)


def _rwkv(pa, mu, w0, w2, a0, a2, g2, k_k, k_a, r_k, lnx_g, lnx_b, *, rows=256):
    b, t, ap = pa.shape
    rows = min(rows, t)
    width = w0.shape[0]
    w_lora, a_lora = w2.shape[0], a2.shape[0]
    assert w_lora == HEAD and a_lora == HEAD and width % LANES == 0
    n_pairs = width // LANES
    wwa = jnp.zeros((w_lora + a_lora, 2 * width), F32)
    wwa = wwa.at[:w_lora, :width].set(w2).at[w_lora:, width:].set(a2)
    ti = jnp.arange(rows)
    tri = ((ti[None, :] <= ti[:, None])
           & (ti[None, :] // CHUNK == ti[:, None] // CHUNK)).astype(BF16)
    hi = jnp.arange(width) // HEAD
    hsum = (hi[:, None] == hi[None, :]).astype(BF16)
    vec = lambda z: z.reshape(1, width)
    return pl.pallas_call(
        _rwkv_body,
        grid=(b, t // rows),
        in_specs=[
            pl.BlockSpec((1, rows, ap), lambda i, c: (i, c, 0)),
            _const_spec((1, ap)),
            _const_spec((1, width)), _const_spec((1, width)), _const_spec((1, width)),
            _const_spec((1, width)), _const_spec((1, width)), _const_spec((1, width)),
            _const_spec((1, width)),
            _const_spec((w_lora + a_lora, 2 * width)),
            _const_spec(g2.shape),
            _const_spec((rows, rows)),
            _const_spec((width, width)),
        ],
        out_specs=pl.BlockSpec((1, rows, width), lambda i, c: (i, c, 0)),
        out_shape=jax.ShapeDtypeStruct((b, t, width), BF16),
        scratch_shapes=[pltpu.VMEM((1, ap), F32), pltpu.VMEM((n_pairs, LANES, LANES), F32)],
        compiler_params=_params(2),
        name="rwkv7",
    )(pa, mu.reshape(1, ap), vec(w0), vec(a0), vec(k_k), vec(k_a), vec(r_k), vec(lnx_g),
      vec(lnx_b), _bf(wwa), _bf(g2), tri, hsum)


def _rglru_body(xb_ref, gb_ref, cw_ref, cb_ref, wg_ref, bg_ref, lam_ref, o_ref,
                prev_scr, h_scr):
    i = pl.program_id(1)
    tt, width = o_ref.shape[1], o_ref.shape[2]

    @pl.when(i == 0)
    def _():
        prev_scr[...] = jnp.zeros_like(prev_scr)
        h_scr[...] = jnp.zeros_like(h_scr)

    x = xb_ref[0]
    prev = prev_scr[...]
    prev_scr[...] = x[tt - 8:, :]
    row8 = lax.broadcasted_iota(jnp.int32, (8, width), 0)

    def delayed(d):
        rolled = pltpu.roll(x, d, axis=0)
        head = jnp.where(row8 < d, pltpu.roll(prev, d, axis=0), rolled[:8])
        return jnp.concatenate([head, rolled[8:]], axis=0)

    xc = cb_ref[...] + cw_ref[CONV_WIDTH - 1:CONV_WIDTH, :] * x
    for d in range(1, CONV_WIDTH):
        xc = xc + cw_ref[CONV_WIDTH - 1 - d:CONV_WIDTH - d, :] * delayed(d)

    gates = _dot(_bf(xc), wg_ref[...]) + bg_ref[...]
    r_g = _sigmoid(gates[:, :width])
    i_g = _sigmoid(gates[:, width:])
    log_a = -LRU_C * r_g * _softplus(-lam_ref[...])
    a = jnp.exp(log_a)
    mult = jnp.sqrt(jnp.tanh(-log_a) * (1.0 + a * a))
    u = mult * i_g * xc

    row = lax.broadcasted_iota(jnp.int32, (tt, width), 0)
    d = 1
    while d < tt:
        keep = row >= d
        a_sh = jnp.where(keep, pltpu.roll(a, d, axis=0), 1.0)
        u_sh = jnp.where(keep, pltpu.roll(u, d, axis=0), 0.0)
        u = a * u_sh + u
        a = a * a_sh
        d *= 2
    h = u + a * h_scr[...]
    h_scr[...] = h[tt - 1:, :]
    o_ref[0] = (h * _gelu_tanh(gb_ref[0])).astype(o_ref.dtype)


def _rglru(xb, gb, conv_w, conv_b, gate_a_w, gate_a_b, gate_x_w, gate_x_b, lam, *, tt=256):
    b, t, width = xb.shape
    tt = min(tt, t)
    nb, bd, _ = gate_a_w.shape
    eye = jnp.eye(nb, dtype=F32)
    dense = lambda w: jnp.einsum("nij,nm->nimj", w, eye).reshape(width, width)
    wg = _bf(jnp.concatenate([dense(gate_a_w), dense(gate_x_w)], axis=1))
    bg = jnp.concatenate([gate_a_b, gate_x_b]).reshape(1, 2 * width)
    tile = pl.BlockSpec((1, tt, width), lambda i, j: (i, j, 0))
    return pl.pallas_call(
        _rglru_body,
        grid=(b, t // tt),
        in_specs=[tile, tile, _const_spec((CONV_WIDTH, width)), _const_spec((1, width)),
                  _const_spec((width, 2 * width)), _const_spec((1, 2 * width)),
                  _const_spec((1, width))],
        out_specs=tile,
        out_shape=jax.ShapeDtypeStruct((b, t, width), BF16),
        scratch_shapes=[pltpu.VMEM((8, width), F32), pltpu.VMEM((1, width), F32)],
        compiler_params=_params(2),
        name="rglru",
    )(xb, gb, conv_w, conv_b.reshape(1, width), wg, bg, lam.reshape(1, width))


def _sb_body(q_ref, k_ref, v_ref, later_ref, o_ref, *, blk):
    qi = pl.program_id(2)
    q = q_ref[0] * (1.0 / math.sqrt(HEAD))
    first = lax.broadcasted_iota(jnp.int32, (blk, LANES), 1) < HEAD
    q2 = jnp.concatenate([jnp.where(first, q, 0), jnp.where(first, 0, q)], axis=0)
    later = later_ref[...]
    kr = lax.broadcasted_iota(jnp.int32, (2 * blk, blk), 0)
    kc = lax.broadcasted_iota(jnp.int32, (2 * blk, blk), 1)
    causal = kc < jnp.where(kr < blk, kr, kr - blk)

    def block(j, carry, masked):
        acc, run = carry
        start = pl.multiple_of(j * blk, blk)
        k = k_ref[0, pl.ds(start, blk), :]
        v = v_ref[0, pl.ds(start, blk), :]
        v2 = jnp.concatenate([jnp.where(first, v, 0), jnp.where(first, 0, v)], axis=0)
        z = _dot_nt(q2, k)
        neg = -(jnp.maximum(z, 0.0) + jnp.log(1.0 + jnp.exp(-jnp.abs(z))))
        if masked:
            neg = jnp.where(causal, neg, 0.0)
        hi = _bf(neg)
        lo = _bf(neg - hi.astype(F32))
        cs = _dot(jnp.concatenate([hi, lo], axis=0), later)
        logit = z + neg + cs[:2 * blk] + cs[2 * blk:] + run
        att = jnp.exp(logit)
        if masked:
            att = jnp.where(causal, att, 0.0)
        att = _bf(att)
        acc = acc + _dot(jnp.concatenate([att[:blk], att[blk:]], axis=1), v2)
        run = run + jnp.sum(neg, axis=-1, keepdims=True)
        return acc, run

    carry = (jnp.zeros((blk, LANES), F32), jnp.zeros((2 * blk, 1), F32))
    carry = block(qi, carry, True)
    carry = lax.fori_loop(0, qi, lambda jj, cr: block(qi - 1 - jj, cr, False), carry)
    o_ref[0] = carry[0].astype(o_ref.dtype)


def _stick_breaking(q, k, v, *, blk=256):
    b, t, width = q.shape
    blk = min(blk, t)
    n_pairs = width // LANES
    ki = jnp.arange(blk)
    later = (ki[:, None] > ki[None, :]).astype(BF16)
    return pl.pallas_call(
        functools.partial(_sb_body, blk=blk),
        grid=(b, n_pairs, t // blk),
        in_specs=[
            pl.BlockSpec((1, blk, LANES), lambda i, p, j: (i, j, p)),
            pl.BlockSpec((1, t, LANES), lambda i, p, j: (i, 0, p)),
            pl.BlockSpec((1, t, LANES), lambda i, p, j: (i, 0, p)),
            _const_spec((blk, blk)),
        ],
        out_specs=pl.BlockSpec((1, blk, LANES), lambda i, p, j: (i, j, p)),
        out_shape=jax.ShapeDtypeStruct((b, t, width), BF16),
        compiler_params=_params(3),
        name="stick_breaking",
    )(q, k, v, later)


def kernel(x, l0_ffn1_pre_g, l0_ffn1_post_g, l0_ffn1_w_in, l0_ffn1_w_out, l0_mix_pre_g, l0_mix_post_g, l0_w_in, l0_mu, l0_w0, l0_w2, l0_a0, l0_a2, l0_g2, l0_k_k, l0_k_a, l0_r_k, l0_lnx_g, l0_lnx_b, l0_conv_w, l0_conv_b, l0_gate_a_w, l0_gate_a_b, l0_gate_x_w, l0_gate_x_b, l0_lambda, l0_w_out, l0_ffn2_pre_g, l0_ffn2_post_g, l0_ffn2_w_in, l0_ffn2_w_out, l1_ffn1_pre_g, l1_ffn1_post_g, l1_ffn1_w_in, l1_ffn1_w_out, l1_mix_pre_g, l1_mix_post_g, l1_w_qkv, l1_w_out, l1_ffn2_pre_g, l1_ffn2_post_g, l1_ffn2_w_in, l1_ffn2_w_out):
    b, t, d = x.shape
    a_width = l0_w0.shape[0]
    b_width = l0_lambda.shape[0]
    a_proj = l0_mu.shape[0]
    h = x.reshape(b * t, d)

    h = _ffn(h, l0_ffn1_pre_g, l0_ffn1_post_g, l0_ffn1_w_in, l0_ffn1_w_out)
    pa, xb, gb = _proj(h, l0_mix_pre_g, l0_w_in, (a_proj, b_width, b_width), F32)
    ya = _rwkv(pa.reshape(b, t, a_proj), l0_mu, l0_w0, l0_w2, l0_a0, l0_a2, l0_g2, l0_k_k,
               l0_k_a, l0_r_k.reshape(-1), l0_lnx_g, l0_lnx_b)
    yb = _rglru(xb.reshape(b, t, b_width), gb.reshape(b, t, b_width), l0_conv_w, l0_conv_b,
                l0_gate_a_w, l0_gate_a_b, l0_gate_x_w, l0_gate_x_b, l0_lambda)
    h = _outproj([ya.reshape(b * t, a_width), yb.reshape(b * t, b_width)],
                 [l0_w_out[:a_width], l0_w_out[a_width:]], h, l0_mix_post_g)
    h = _ffn(h, l0_ffn2_pre_g, l0_ffn2_post_g, l0_ffn2_w_in, l0_ffn2_w_out)

    h = _ffn(h, l1_ffn1_pre_g, l1_ffn1_post_g, l1_ffn1_w_in, l1_ffn1_w_out)
    q, k, v = _proj(h, l1_mix_pre_g, l1_w_qkv, (d, d, d), BF16)
    o = _stick_breaking(q.reshape(b, t, d), k.reshape(b, t, d), v.reshape(b, t, d))
    h = _outproj([o.reshape(b * t, d)], [l1_w_out], h, l1_mix_post_g)
    h = _ffn(h, l1_ffn2_pre_g, l1_ffn2_post_g, l1_ffn2_w_in, l1_ffn2_w_out)
    return h.reshape(b, t, d)
```

```python
import functools
import math

import jax
import jax.numpy as jnp
from jax import lax
from jax.experimental import pallas as pl
from jax.experimental.pallas import tpu as pltpu

F32 = jnp.float32
BF16 = jnp.bfloat16

NORM_EPS = 1e-6
GN_EPS = 64e-5
LRU_C = 8.0
HEAD = 64
LANES = 128
CHUNK = 64
CONV_WIDTH = 4
LOG2E = 1.0 / math.log(2.0)
SB_SPLIT = 2
VMEM_LIMIT = 56 * 1024 * 1024


def _params(n_axes, vmem=VMEM_LIMIT):
    return pltpu.CompilerParams(
        dimension_semantics=("arbitrary",) * n_axes, vmem_limit_bytes=vmem)


def _bf(x):
    return x.astype(BF16)


def _dot(a, b):
    return jnp.dot(a, b, preferred_element_type=F32)


def _dot_nt(a, b):
    return lax.dot_general(a, b, (((1,), (1,)), ((), ())), preferred_element_type=F32)


def _dot_hilo(x, e):
    hi = _bf(x)
    lo = _bf(x - hi.astype(F32))
    return _dot(hi, e) + _dot(lo, e)


def _rms(x, g):
    return x * lax.rsqrt(jnp.mean(x * x, axis=-1, keepdims=True) + NORM_EPS) * g


def _sigmoid(x):
    return 1.0 / (1.0 + jnp.exp(-x))


def _softplus(x):
    return jnp.maximum(x, 0.0) + jnp.log1p(jnp.exp(-jnp.abs(x)))


def _gelu_tanh(x):
    c = math.sqrt(2.0 / math.pi)
    return 0.5 * x * (1.0 + jnp.tanh(c * (x + 0.044715 * (x * x * x))))


def _row_spec(tm, n):
    return pl.BlockSpec((tm, n), lambda i, *_: (i, 0))


def _const_spec(shape):
    return pl.BlockSpec(shape, lambda *_: (0,) * len(shape))


def _ffn_body(x_ref, gpre_ref, gpost_ref, win_ref, wo_ref, o_ref, *, tf):
    f = wo_ref.shape[0]
    x = x_ref[...]
    h = _bf(_rms(x, gpre_ref[...]))

    def gate_up(c):
        return (_dot(h, win_ref[:, c * tf:(c + 1) * tf]),
                _dot(h, win_ref[:, f + c * tf:f + (c + 1) * tf]))

    n = f // tf
    acc = None
    gate, up = gate_up(0)
    for c in range(n):
        act = _bf(gate * _sigmoid(gate) * up)
        if c + 1 < n:
            gate, up = gate_up(c + 1)
        part = _dot(act, wo_ref[c * tf:(c + 1) * tf, :])
        acc = part if acc is None else acc + part
    o_ref[...] = x + 0.5 * _rms(acc, gpost_ref[...])


def _ffn(x, g_pre, g_post, w_in, w_out, *, tm=512, tf=256):
    m, d = x.shape
    tm = min(tm, m)
    f = w_out.shape[0]
    resident = lambda shape: pl.BlockSpec(shape, lambda i: (0, 0), pipeline_mode=pl.Buffered(1))
    return pl.pallas_call(
        functools.partial(_ffn_body, tf=tf),
        grid=(m // tm,),
        in_specs=[_row_spec(tm, d), _const_spec((1, d)), _const_spec((1, d)),
                  resident((d, 2 * f)), resident((f, d))],
        out_specs=_row_spec(tm, d),
        out_shape=jax.ShapeDtypeStruct((m, d), F32),
        compiler_params=_params(1),
        name="ffn",
    )(x, g_pre.reshape(1, d), g_post.reshape(1, d), _bf(w_in), _bf(w_out))


def _proj_body(x_ref, g_ref, w_ref, *o_refs, scales):
    h = _bf(_rms(x_ref[...], g_ref[...]))
    y = _dot(h, w_ref[...])
    off = 0
    for o_ref, scale in zip(o_refs, scales):
        n = o_ref.shape[-1]
        part = y[:, off:off + n]
        if scale is not None:
            part = part * scale
        o_ref[...] = part.astype(o_ref.dtype)
        off += n


def _proj(x, g, w, splits, dtype, *, scales=None, tm=512):
    m, d = x.shape
    tm = min(tm, m)
    n = w.shape[1]
    assert sum(splits) == n
    scales = scales or (None,) * len(splits)
    return pl.pallas_call(
        functools.partial(_proj_body, scales=scales),
        grid=(m // tm,),
        in_specs=[_row_spec(tm, d), _const_spec((1, d)), _const_spec((d, n))],
        out_specs=[_row_spec(tm, s) for s in splits],
        out_shape=[jax.ShapeDtypeStruct((m, s), dtype) for s in splits],
        compiler_params=_params(1),
        name="proj",
    )(x, g.reshape(1, d), _bf(w))


def _outproj_body(*refs, n_in):
    a_refs = refs[:n_in]
    w_refs = refs[n_in:2 * n_in]
    res_ref, g_ref, o_ref = refs[2 * n_in:]
    y = _dot(a_refs[0][...], w_refs[0][...])
    for a_ref, w_ref in zip(a_refs[1:], w_refs[1:]):
        y += _dot(a_ref[...], w_ref[...])
    o_ref[...] = res_ref[...] + _rms(y, g_ref[...])


def _outproj(acts, ws, res, g, *, tm=512):
    m, d = res.shape
    tm = min(tm, m)
    n_in = len(acts)
    return pl.pallas_call(
        functools.partial(_outproj_body, n_in=n_in),
        grid=(m // tm,),
        in_specs=([_row_spec(tm, a.shape[1]) for a in acts]
                  + [_const_spec(w.shape) for w in ws]
                  + [_row_spec(tm, d), _const_spec((1, d))]),
        out_specs=_row_spec(tm, d),
        out_shape=jax.ShapeDtypeStruct((m, d), F32),
        compiler_params=_params(1),
        name="outproj",
    )(*acts, *[_bf(w) for w in ws], res, g.reshape(1, d))


def _solve_unit_lower(ns, xs):
    ps = [_bf(n) for n in ns]
    steps = int(math.log2(CHUNK))
    for i in range(steps):
        xs = [x + _dot(p, _bf(x)) for p, x in zip(ps, xs)]
        if i + 1 < steps:
            ps = [_bf(_dot(p, p)) for p in ps]
    return xs


def _rwkv_body(pa_ref, mu_ref, w0_ref, a0_ref, kk_ref, ka_ref, rk_ref, lng_ref, lnb_ref,
               wwa_ref, g2_ref, tri_ref, hsum_ref, o_ref, prev_scr, s_scr):
    step = pl.program_id(1)
    rows, width = o_ref.shape[1], o_ref.shape[2]
    n_pairs = width // LANES
    n_chunks = rows // CHUNK

    @pl.when(step == 0)
    def _():
        prev_scr[...] = jnp.zeros_like(prev_scr)
        s_scr[...] = jnp.zeros_like(s_scr)

    p = pa_ref[0]
    row = lax.broadcasted_iota(jnp.int32, p.shape, 0)
    shifted = jnp.where(row == 0, prev_scr[...], pltpu.roll(p, 1, axis=0))
    prev_scr[...] = p[rows - 1:rows, :]
    xm = p + mu_ref[...] * (shifted - p)

    r = xm[:, 0:width]
    k = xm[:, width:2 * width]
    v = xm[:, 2 * width:3 * width]
    wa_lo = xm[:, 3 * width:3 * width + LANES]
    g_lo = xm[:, 3 * width + LANES:]
    lane_wa = lax.broadcasted_iota(jnp.int32, wa_lo.shape, 1)
    wa_act = jnp.where(lane_wa < LANES // 2, jnp.tanh(wa_lo), wa_lo)
    lora = _dot(_bf(wa_act), wwa_ref[...])
    w_log = -_softplus(-(w0_ref[...] + lora[:, :width])) - 0.5
    log_decay = -jnp.exp(w_log)
    a = _sigmoid(a0_ref[...] + lora[:, width:])
    gate = _dot(_bf(_sigmoid(g_lo)), g2_ref[...])

    hsum = hsum_ref[...]
    kk = k * kk_ref[...]
    kk = kk / jnp.maximum(jnp.sqrt(_dot_hilo(kk * kk, hsum)), 1e-12)
    k_h = k * (1.0 + (a - 1.0) * ka_ref[...])

    ld_hi = _bf(log_decay)
    ld_lo = _bf(log_decay - ld_hi.astype(F32))
    tri = tri_ref[...]
    cum = _dot(tri, ld_hi) + _dot(tri, ld_lo)
    dec_in = jnp.exp(cum)
    dec_out = jnp.exp(-cum)
    r_f = r * dec_in
    a_f = -kk * jnp.exp(cum - log_decay)
    b_t = _bf(kk * a * dec_out)
    k_t = _bf(k_h * dec_out)

    lane = lax.broadcasted_iota(jnp.int32, (CHUNK, LANES), 1)
    trow = lax.broadcasted_iota(jnp.int32, (CHUNK, LANES), 0)
    first = lane < HEAD
    src = jnp.where(first, lane, lane - HEAD)
    strict = src < trow
    incl = src <= trow
    lane2 = lax.broadcasted_iota(jnp.int32, (CHUNK, 2 * LANES), 1)
    first2 = jnp.where(lane2 < LANES, lane2, lane2 - LANES) < HEAD
    sr = lax.broadcasted_iota(jnp.int32, (LANES, LANES), 0)
    sc = lax.broadcasted_iota(jnp.int32, (LANES, LANES), 1)
    same_head = (sr < HEAD) == (sc < HEAD)
    z16 = jnp.zeros((CHUNK, LANES), BF16)
    z32 = jnp.zeros((CHUNK, LANES), F32)

    inst = [(c, pr) for c in range(n_chunks) for pr in range(n_pairs)]
    blk = lambda z, c, pr: z[c * CHUNK:(c + 1) * CHUNK, pr * LANES:(pr + 1) * LANES]
    afs = [blk(a_f, c, pr) for c, pr in inst]
    rfs = [blk(r_f, c, pr) for c, pr in inst]
    vfs = [blk(v, c, pr) for c, pr in inst]
    vts = [_bf(vf) for vf in vfs]
    bks = [jnp.concatenate([blk(b_t, c, pr), blk(k_t, c, pr)], axis=0) for c, pr in inst]
    kbs = [jnp.concatenate([blk(k_t, c, pr), blk(b_t, c, pr)], axis=0) for c, pr in inst]
    gh0s = [_dot_nt(jnp.concatenate([jnp.where(first, _bf(af), 0), jnp.where(first, _bf(rf), 0)],
                                    axis=0), bk) for af, rf, bk in zip(afs, rfs, bks)]
    gh1s = [_dot_nt(jnp.concatenate([jnp.where(first, 0, _bf(af)), jnp.where(first, 0, _bf(rf))],
                                    axis=0), kb) for af, rf, kb in zip(afs, rfs, kbs)]
    g0s = [jnp.where(strict, gh[:CHUNK], 0.0) for gh in gh0s]
    g1s = [jnp.where(strict, gh[:CHUNK], 0.0) for gh in gh1s]
    lhss = [jnp.concatenate(
        [jnp.concatenate([_bf(jnp.where(incl, ga[CHUNK:], 0.0)), z16], axis=1),
         jnp.concatenate([z16, _bf(jnp.where(incl, gb[CHUNK:], 0.0))], axis=1)], axis=0)
        for ga, gb in zip(gh0s, gh1s)]
    n_blks = [jnp.concatenate([jnp.where(first, g0, 0.0), jnp.where(first, 0.0, g1)], axis=0)
              for g0, g1 in zip(g0s, g1s)]
    akvs = [_dot(_bf(jnp.concatenate([jnp.where(first, 0.0, g0), jnp.where(first, g1, 0.0)],
                                     axis=0)), jnp.concatenate([vt, vt], axis=0))
            for g0, g1, vt in zip(g0s, g1s, vts)]
    xs = [jnp.concatenate([jnp.concatenate([af, af], axis=0), akv], axis=1)
          for af, akv in zip(afs, akvs)]
    xs = _solve_unit_lower(n_blks, xs)
    lows = [jnp.concatenate([jnp.where(first2, x[:CHUNK], x[CHUNK:]),
                             jnp.concatenate([z32, vf], axis=1)], axis=0)
            for x, vf in zip(xs, vfs)]
    lowbs = [_bf(low) for low in lows]
    outs = [_dot(lhs, jnp.concatenate([lowb, lowb[CHUNK:], lowb[:CHUNK]], axis=0))
            for lhs, lowb in zip(lhss, lowbs)]
    sels = [jnp.where(first2, out[:CHUNK], out[CHUNK:]) for out in outs]
    mcs = [_dot(_bf(low.T), bk) for low, bk in zip(lows, bks)]

    states = [s_scr[pr] for pr in range(n_pairs)]
    y_rows = []
    for c in range(n_chunks):
        y_pairs = []
        for pr in range(n_pairs):
            i = c * n_pairs + pr
            w_last = dec_in[(c + 1) * CHUNK - 1:(c + 1) * CHUNK, pr * LANES:(pr + 1) * LANES]
            g_tilde = rfs[i] + sels[i][:, :LANES]
            y_intra = sels[i][:, LANES:]
            m_lr = jnp.where(same_head, mcs[i][:LANES], 0.0) * w_last
            c_mat = jnp.where(same_head, mcs[i][LANES:], 0.0) * w_last
            s = states[pr]
            sb = _bf(s)
            y_pairs.append(_dot_nt(_bf(g_tilde), sb) + y_intra)
            states[pr] = s * w_last + _dot(sb, _bf(m_lr)) + c_mat
        y_rows.append(jnp.concatenate(y_pairs, axis=-1))
    for pr in range(n_pairs):
        s_scr[pr] = states[pr]
    y = jnp.concatenate(y_rows, axis=0)

    inv_n = 1.0 / HEAD
    mean = _dot_hilo(y, hsum) * inv_n
    dev = y - mean
    var = _dot_hilo(dev * dev, hsum) * inv_n
    yn = dev * lax.rsqrt(var + GN_EPS) * lng_ref[...] + lnb_ref[...]
    bonus = _dot_hilo(r * k_h * rk_ref[...], hsum) * v
    o_ref[0] = ((yn + bonus) * gate).astype(o_ref.dtype)


def _rwkv(pa, mu, w0, w2, a0, a2, g2, k_k, k_a, r_k, lnx_g, lnx_b, *, rows=256):
    b, t, ap = pa.shape
    rows = min(rows, t)
    width = w0.shape[0]
    w_lora, a_lora = w2.shape[0], a2.shape[0]
    assert w_lora == HEAD and a_lora == HEAD and width % LANES == 0
    n_pairs = width // LANES
    wwa = jnp.zeros((w_lora + a_lora, 2 * width), F32)
    wwa = wwa.at[:w_lora, :width].set(w2).at[w_lora:, width:].set(a2)
    ti = jnp.arange(rows)
    tri = ((ti[None, :] <= ti[:, None])
           & (ti[None, :] // CHUNK == ti[:, None] // CHUNK)).astype(BF16)
    hi = jnp.arange(width) // HEAD
    hsum = (hi[:, None] == hi[None, :]).astype(BF16)
    vec = lambda z: z.reshape(1, width)
    return pl.pallas_call(
        _rwkv_body,
        grid=(b, t // rows),
        in_specs=[
            pl.BlockSpec((1, rows, ap), lambda i, c: (i, c, 0)),
            _const_spec((1, ap)),
            _const_spec((1, width)), _const_spec((1, width)), _const_spec((1, width)),
            _const_spec((1, width)), _const_spec((1, width)), _const_spec((1, width)),
            _const_spec((1, width)),
            _const_spec((w_lora + a_lora, 2 * width)),
            _const_spec(g2.shape),
            _const_spec((rows, rows)),
            _const_spec((width, width)),
        ],
        out_specs=pl.BlockSpec((1, rows, width), lambda i, c: (i, c, 0)),
        out_shape=jax.ShapeDtypeStruct((b, t, width), BF16),
        scratch_shapes=[pltpu.VMEM((1, ap), F32), pltpu.VMEM((n_pairs, LANES, LANES), F32)],
        compiler_params=_params(2),
        name="rwkv7",
    )(pa, mu.reshape(1, ap), vec(w0), vec(a0), vec(k_k), vec(k_a), vec(r_k), vec(lnx_g),
      vec(lnx_b), _bf(wwa), _bf(g2), tri, hsum)


def _rglru_body(xb_ref, gb_ref, cw_ref, cb_ref, wg_ref, bg_ref, lam_ref, o_ref,
                prev_scr, h_scr):
    i = pl.program_id(1)
    tt, width = o_ref.shape[1], o_ref.shape[2]

    @pl.when(i == 0)
    def _():
        prev_scr[...] = jnp.zeros_like(prev_scr)
        h_scr[...] = jnp.zeros_like(h_scr)

    x = xb_ref[0]
    prev = prev_scr[...]
    prev_scr[...] = x[tt - 8:, :]
    row8 = lax.broadcasted_iota(jnp.int32, (8, width), 0)

    def delayed(d):
        rolled = pltpu.roll(x, d, axis=0)
        head = jnp.where(row8 < d, pltpu.roll(prev, d, axis=0), rolled[:8])
        return jnp.concatenate([head, rolled[8:]], axis=0)

    xc = cb_ref[...] + cw_ref[CONV_WIDTH - 1:CONV_WIDTH, :] * x
    for d in range(1, CONV_WIDTH):
        xc = xc + cw_ref[CONV_WIDTH - 1 - d:CONV_WIDTH - d, :] * delayed(d)

    gates = _dot(_bf(xc), wg_ref[...]) + bg_ref[...]
    r_g = _sigmoid(gates[:, :width])
    i_g = _sigmoid(gates[:, width:])
    log_a = -LRU_C * r_g * _softplus(-lam_ref[...])
    a = jnp.exp(log_a)
    mult = jnp.sqrt(jnp.tanh(-log_a) * (1.0 + a * a))
    u = mult * i_g * xc

    row = lax.broadcasted_iota(jnp.int32, (tt, width), 0)
    d = 1
    while d < tt:
        keep = row >= d
        a_sh = jnp.where(keep, pltpu.roll(a, d, axis=0), 1.0)
        u_sh = jnp.where(keep, pltpu.roll(u, d, axis=0), 0.0)
        u = a * u_sh + u
        a = a * a_sh
        d *= 2
    h = u + a * h_scr[...]
    h_scr[...] = h[tt - 1:, :]
    o_ref[0] = (h * _gelu_tanh(gb_ref[0])).astype(o_ref.dtype)


def _rglru(xb, gb, conv_w, conv_b, gate_a_w, gate_a_b, gate_x_w, gate_x_b, lam, *, tt=256):
    b, t, width = xb.shape
    tt = min(tt, t)
    nb, bd, _ = gate_a_w.shape
    eye = jnp.eye(nb, dtype=F32)
    dense = lambda w: jnp.einsum("nij,nm->nimj", w, eye).reshape(width, width)
    wg = _bf(jnp.concatenate([dense(gate_a_w), dense(gate_x_w)], axis=1))
    bg = jnp.concatenate([gate_a_b, gate_x_b]).reshape(1, 2 * width)
    tile = pl.BlockSpec((1, tt, width), lambda i, j: (i, j, 0))
    return pl.pallas_call(
        _rglru_body,
        grid=(b, t // tt),
        in_specs=[tile, tile, _const_spec((CONV_WIDTH, width)), _const_spec((1, width)),
                  _const_spec((width, 2 * width)), _const_spec((1, 2 * width)),
                  _const_spec((1, width))],
        out_specs=tile,
        out_shape=jax.ShapeDtypeStruct((b, t, width), BF16),
        scratch_shapes=[pltpu.VMEM((8, width), F32), pltpu.VMEM((1, width), F32)],
        compiler_params=_params(2),
        name="rglru",
    )(xb, gb, conv_w, conv_b.reshape(1, width), wg, bg, lam.reshape(1, width))


def _sb_body(q_ref, k_ref, v_ref, later_ref, o_ref, *, blk):
    qi = pl.program_id(2)
    q = q_ref[0]
    first = lax.broadcasted_iota(jnp.int32, (blk, LANES), 1) < HEAD
    sub = blk // SB_SPLIT
    streams = [(h, r) for h in range(2) for r in range(SB_SPLIT)]
    qh = (jnp.where(first, q, 0), jnp.where(first, 0, q))
    qs = [qh[h][r * sub:(r + 1) * sub] for h, r in streams]
    later = later_ref[...]
    kr = lax.broadcasted_iota(jnp.int32, (sub, blk), 0)
    kc = lax.broadcasted_iota(jnp.int32, (sub, blk), 1)
    causal = [kc < kr + r * sub for _, r in streams]

    def block(j, carry, masked):
        accs, runs = carry
        start = pl.multiple_of(j * blk, blk)
        k = k_ref[0, pl.ds(start, blk), :]
        v = v_ref[0, pl.ds(start, blk), :]
        zns = [_dot_nt(q, k) for q in qs]
        lows = [jnp.minimum(zn, 0.0) for zn in zns]
        tops = [low - zn for low, zn in zip(lows, zns)]
        lses = [jnp.log(1.0 + jnp.exp2(low + top)) * LOG2E
                for low, top in zip(lows, tops)]
        negs = [low - lse for low, lse in zip(lows, lses)]
        if masked:
            negs = [jnp.where(keep, neg, 0.0) for neg, keep in zip(negs, causal)]
        css = [_dot(_bf(neg), later) for neg in negs]
        atts = [jnp.exp2((top - lse) + cs + run)
                for top, lse, cs, run in zip(tops, lses, css, runs)]
        if masked:
            atts = [jnp.where(keep, att, 0.0) for att, keep in zip(atts, causal)]
        accs = [acc + _dot(_bf(att), v) for acc, att in zip(accs, atts)]
        runs = [run + jnp.sum(neg, axis=-1, keepdims=True) for run, neg in zip(runs, negs)]
        return accs, runs

    carry = ([jnp.zeros((sub, LANES), F32) for _ in streams],
             [jnp.zeros((sub, 1), F32) for _ in streams])
    carry = block(qi, carry, True)
    accs, _ = lax.fori_loop(0, qi, lambda jj, cr: block(qi - 1 - jj, cr, False), carry)
    halves = [jnp.concatenate(accs[h * SB_SPLIT:(h + 1) * SB_SPLIT], axis=0) for h in range(2)]
    o_ref[0] = jnp.where(first, halves[0], halves[1]).astype(o_ref.dtype)


def _stick_breaking(q, k, v, *, blk=256):
    b, t, width = q.shape
    blk = min(blk, t)
    n_pairs = width // LANES
    ki = jnp.arange(blk)
    later = (ki[:, None] > ki[None, :]).astype(BF16)
    return pl.pallas_call(
        functools.partial(_sb_body, blk=blk),
        grid=(b, n_pairs, t // blk),
        in_specs=[
            pl.BlockSpec((1, blk, LANES), lambda i, p, j: (i, j, p)),
            pl.BlockSpec((1, t, LANES), lambda i, p, j: (i, 0, p)),
            pl.BlockSpec((1, t, LANES), lambda i, p, j: (i, 0, p)),
            _const_spec((blk, blk)),
        ],
        out_specs=pl.BlockSpec((1, blk, LANES), lambda i, p, j: (i, j, p)),
        out_shape=jax.ShapeDtypeStruct((b, t, width), BF16),
        compiler_params=_params(3),
        name="stick_breaking",
    )(q, k, v, later)


def kernel(x, l0_ffn1_pre_g, l0_ffn1_post_g, l0_ffn1_w_in, l0_ffn1_w_out, l0_mix_pre_g, l0_mix_post_g, l0_w_in, l0_mu, l0_w0, l0_w2, l0_a0, l0_a2, l0_g2, l0_k_k, l0_k_a, l0_r_k, l0_lnx_g, l0_lnx_b, l0_conv_w, l0_conv_b, l0_gate_a_w, l0_gate_a_b, l0_gate_x_w, l0_gate_x_b, l0_lambda, l0_w_out, l0_ffn2_pre_g, l0_ffn2_post_g, l0_ffn2_w_in, l0_ffn2_w_out, l1_ffn1_pre_g, l1_ffn1_post_g, l1_ffn1_w_in, l1_ffn1_w_out, l1_mix_pre_g, l1_mix_post_g, l1_w_qkv, l1_w_out, l1_ffn2_pre_g, l1_ffn2_post_g, l1_ffn2_w_in, l1_ffn2_w_out):
    b, t, d = x.shape
    a_width = l0_w0.shape[0]
    b_width = l0_lambda.shape[0]
    a_proj = l0_mu.shape[0]
    h = x.reshape(b * t, d)

    h = _ffn(h, l0_ffn1_pre_g, l0_ffn1_post_g, l0_ffn1_w_in, l0_ffn1_w_out)
    pa, xb, gb = _proj(h, l0_mix_pre_g, l0_w_in, (a_proj, b_width, b_width), F32)
    ya = _rwkv(pa.reshape(b, t, a_proj), l0_mu, l0_w0, l0_w2, l0_a0, l0_a2, l0_g2, l0_k_k,
               l0_k_a, l0_r_k.reshape(-1), l0_lnx_g, l0_lnx_b)
    yb = _rglru(xb.reshape(b, t, b_width), gb.reshape(b, t, b_width), l0_conv_w, l0_conv_b,
                l0_gate_a_w, l0_gate_a_b, l0_gate_x_w, l0_gate_x_b, l0_lambda)
    h = _outproj([ya.reshape(b * t, a_width), yb.reshape(b * t, b_width)],
                 [l0_w_out[:a_width], l0_w_out[a_width:]], h, l0_mix_post_g)
    h = _ffn(h, l0_ffn2_pre_g, l0_ffn2_post_g, l0_ffn2_w_in, l0_ffn2_w_out)

    h = _ffn(h, l1_ffn1_pre_g, l1_ffn1_post_g, l1_ffn1_w_in, l1_ffn1_w_out)
    q, k, v = _proj(h, l1_mix_pre_g, l1_w_qkv, (d, d, d), BF16,
                    scales=(-LOG2E / math.sqrt(HEAD), None, None))
    o = _stick_breaking(q.reshape(b, t, d), k.reshape(b, t, d), v.reshape(b, t, d))
    h = _outproj([o.reshape(b * t, d)], [l1_w_out], h, l1_mix_post_g)
    h = _ffn(h, l1_ffn2_pre_g, l1_ffn2_post_g, l1_ffn2_w_in, l1_ffn2_w_out)
    return h.reshape(b, t, d)
```

```python
import functools
import math

import jax
import jax.numpy as jnp
from jax import lax
from jax.experimental import pallas as pl
from jax.experimental.pallas import tpu as pltpu

F32 = jnp.float32
BF16 = jnp.bfloat16

NORM_EPS = 1e-6
GN_EPS = 64e-5
LRU_C = 8.0
HEAD = 64
LANES = 128
CHUNK = 64
CONV_WIDTH = 4
LOG2E = 1.0 / math.log(2.0)
SB_SPLIT = 2
VMEM_LIMIT = 56 * 1024 * 1024


def _params(n_axes, vmem=VMEM_LIMIT):
    return pltpu.CompilerParams(
        dimension_semantics=("arbitrary",) * n_axes, vmem_limit_bytes=vmem)


def _bf(x):
    return x.astype(BF16)


def _dot(a, b):
    return jnp.dot(a, b, preferred_element_type=F32)


def _dot_nt(a, b):
    return lax.dot_general(a, b, (((1,), (1,)), ((), ())), preferred_element_type=F32)


def _dot_hilo(x, e):
    hi = _bf(x)
    lo = _bf(x - hi.astype(F32))
    return _dot(hi, e) + _dot(lo, e)


def _rms(x, g):
    return x * lax.rsqrt(jnp.mean(x * x, axis=-1, keepdims=True) + NORM_EPS) * g


def _sigmoid(x):
    return 1.0 / (1.0 + jnp.exp(-x))


def _softplus(x):
    return jnp.maximum(x, 0.0) + jnp.log1p(jnp.exp(-jnp.abs(x)))


def _gelu_tanh(x):
    c = math.sqrt(2.0 / math.pi)
    return 0.5 * x * (1.0 + jnp.tanh(c * (x + 0.044715 * (x * x * x))))


def _row_spec(tm, n):
    return pl.BlockSpec((tm, n), lambda i, *_: (i, 0))


def _const_spec(shape):
    return pl.BlockSpec(shape, lambda *_: (0,) * len(shape))


def _ffn_body(x_ref, gpre_ref, gpost_ref, win_ref, wo_ref, o_ref, *, tf):
    f = wo_ref.shape[0]
    x = x_ref[...]
    h = _bf(_rms(x, gpre_ref[...]))

    def gate_up(c):
        return (_dot(h, win_ref[:, c * tf:(c + 1) * tf]),
                _dot(h, win_ref[:, f + c * tf:f + (c + 1) * tf]))

    n = f // tf
    acc = None
    gate, up = gate_up(0)
    for c in range(n):
        act = _bf(gate * _sigmoid(gate) * up)
        if c + 1 < n:
            gate, up = gate_up(c + 1)
        part = _dot(act, wo_ref[c * tf:(c + 1) * tf, :])
        acc = part if acc is None else acc + part
    o_ref[...] = x + 0.5 * _rms(acc, gpost_ref[...])


def _ffn(x, g_pre, g_post, w_in, w_out, *, tm=512, tf=256):
    m, d = x.shape
    tm = min(tm, m)
    f = w_out.shape[0]
    resident = lambda shape: pl.BlockSpec(shape, lambda i: (0, 0), pipeline_mode=pl.Buffered(1))
    return pl.pallas_call(
        functools.partial(_ffn_body, tf=tf),
        grid=(m // tm,),
        in_specs=[_row_spec(tm, d), _const_spec((1, d)), _const_spec((1, d)),
                  resident((d, 2 * f)), resident((f, d))],
        out_specs=_row_spec(tm, d),
        out_shape=jax.ShapeDtypeStruct((m, d), F32),
        compiler_params=_params(1),
        name="ffn",
    )(x, g_pre.reshape(1, d), g_post.reshape(1, d), _bf(w_in), _bf(w_out))


def _proj_body(x_ref, g_ref, w_ref, *o_refs, scales):
    h = _bf(_rms(x_ref[...], g_ref[...]))
    y = _dot(h, w_ref[...])
    off = 0
    for o_ref, scale in zip(o_refs, scales):
        n = o_ref.shape[-1]
        part = y[:, off:off + n]
        if scale is not None:
            part = part * scale
        o_ref[...] = part.astype(o_ref.dtype)
        off += n


def _proj(x, g, w, splits, dtype, *, scales=None, tm=512):
    m, d = x.shape
    tm = min(tm, m)
    n = w.shape[1]
    assert sum(splits) == n
    scales = scales or (None,) * len(splits)
    return pl.pallas_call(
        functools.partial(_proj_body, scales=scales),
        grid=(m // tm,),
        in_specs=[_row_spec(tm, d), _const_spec((1, d)), _const_spec((d, n))],
        out_specs=[_row_spec(tm, s) for s in splits],
        out_shape=[jax.ShapeDtypeStruct((m, s), dtype) for s in splits],
        compiler_params=_params(1),
        name="proj",
    )(x, g.reshape(1, d), _bf(w))


def _outproj_body(*refs, n_in):
    a_refs = refs[:n_in]
    w_refs = refs[n_in:2 * n_in]
    res_ref, g_ref, o_ref = refs[2 * n_in:]
    y = _dot(a_refs[0][...], w_refs[0][...])
    for a_ref, w_ref in zip(a_refs[1:], w_refs[1:]):
        y += _dot(a_ref[...], w_ref[...])
    o_ref[...] = res_ref[...] + _rms(y, g_ref[...])


def _outproj(acts, ws, res, g, *, tm=512):
    m, d = res.shape
    tm = min(tm, m)
    n_in = len(acts)
    return pl.pallas_call(
        functools.partial(_outproj_body, n_in=n_in),
        grid=(m // tm,),
        in_specs=([_row_spec(tm, a.shape[1]) for a in acts]
                  + [_const_spec(w.shape) for w in ws]
                  + [_row_spec(tm, d), _const_spec((1, d))]),
        out_specs=_row_spec(tm, d),
        out_shape=jax.ShapeDtypeStruct((m, d), F32),
        compiler_params=_params(1),
        name="outproj",
    )(*acts, *[_bf(w) for w in ws], res, g.reshape(1, d))


def _solve_unit_lower(ns, xs):
    ps = [_bf(n) for n in ns]
    steps = int(math.log2(CHUNK))
    for i in range(steps):
        xs = [x + _dot(p, _bf(x)) for p, x in zip(ps, xs)]
        if i + 1 < steps:
            ps = [_bf(_dot(p, p)) for p in ps]
    return xs


def _rwkv_body(pa_ref, mu_ref, w0_ref, a0_ref, kk_ref, ka_ref, rk_ref, lng_ref, lnb_ref,
               wwa_ref, g2_ref, tri_ref, hsum_ref, o_ref, prev_scr, s_scr):
    step = pl.program_id(1)
    rows, width = o_ref.shape[1], o_ref.shape[2]
    n_pairs = width // LANES
    n_chunks = rows // CHUNK

    @pl.when(step == 0)
    def _():
        prev_scr[...] = jnp.zeros_like(prev_scr)
        s_scr[...] = jnp.zeros_like(s_scr)

    p = pa_ref[0]
    row = lax.broadcasted_iota(jnp.int32, p.shape, 0)
    shifted = jnp.where(row == 0, prev_scr[...], pltpu.roll(p, 1, axis=0))
    prev_scr[...] = p[rows - 1:rows, :]
    xm = p + mu_ref[...] * (shifted - p)

    r = xm[:, 0:width]
    k = xm[:, width:2 * width]
    v = xm[:, 2 * width:3 * width]
    wa_lo = xm[:, 3 * width:3 * width + LANES]
    g_lo = xm[:, 3 * width + LANES:]
    lane_wa = lax.broadcasted_iota(jnp.int32, wa_lo.shape, 1)
    wa_act = jnp.where(lane_wa < LANES // 2, jnp.tanh(wa_lo), wa_lo)
    lora = _dot(_bf(wa_act), wwa_ref[...])
    w_log = -_softplus(-(w0_ref[...] + lora[:, :width])) - 0.5
    log_decay = -jnp.exp(w_log)
    a = _sigmoid(a0_ref[...] + lora[:, width:])
    gate = _dot(_bf(_sigmoid(g_lo)), g2_ref[...])

    hsum = hsum_ref[...]
    kk = k * kk_ref[...]
    kk = kk / jnp.maximum(jnp.sqrt(_dot_hilo(kk * kk, hsum)), 1e-12)
    k_h = k * (1.0 + (a - 1.0) * ka_ref[...])

    ld_hi = _bf(log_decay)
    ld_lo = _bf(log_decay - ld_hi.astype(F32))
    tri = tri_ref[...]
    cum = _dot(tri, ld_hi) + _dot(tri, ld_lo)
    dec_in = jnp.exp(cum)
    dec_out = jnp.exp(-cum)
    r_f = r * dec_in
    a_f = -kk * jnp.exp(cum - log_decay)
    b_t = _bf(kk * a * dec_out)
    k_t = _bf(k_h * dec_out)

    lane = lax.broadcasted_iota(jnp.int32, (CHUNK, LANES), 1)
    trow = lax.broadcasted_iota(jnp.int32, (CHUNK, LANES), 0)
    first = lane < HEAD
    src = jnp.where(first, lane, lane - HEAD)
    strict = src < trow
    incl = src <= trow
    lane2 = lax.broadcasted_iota(jnp.int32, (CHUNK, 2 * LANES), 1)
    first2 = jnp.where(lane2 < LANES, lane2, lane2 - LANES) < HEAD
    sr = lax.broadcasted_iota(jnp.int32, (LANES, LANES), 0)
    sc = lax.broadcasted_iota(jnp.int32, (LANES, LANES), 1)
    same_head = (sr < HEAD) == (sc < HEAD)
    z16 = jnp.zeros((CHUNK, LANES), BF16)
    z32 = jnp.zeros((CHUNK, LANES), F32)

    inst = [(c, pr) for c in range(n_chunks) for pr in range(n_pairs)]
    blk = lambda z, c, pr: z[c * CHUNK:(c + 1) * CHUNK, pr * LANES:(pr + 1) * LANES]
    afs = [blk(a_f, c, pr) for c, pr in inst]
    rfs = [blk(r_f, c, pr) for c, pr in inst]
    vfs = [blk(v, c, pr) for c, pr in inst]
    vts = [_bf(vf) for vf in vfs]
    bks = [jnp.concatenate([blk(b_t, c, pr), blk(k_t, c, pr)], axis=0) for c, pr in inst]
    kbs = [jnp.concatenate([blk(k_t, c, pr), blk(b_t, c, pr)], axis=0) for c, pr in inst]
    gh0s = [_dot_nt(jnp.concatenate([jnp.where(first, _bf(af), 0), jnp.where(first, _bf(rf), 0)],
                                    axis=0), bk) for af, rf, bk in zip(afs, rfs, bks)]
    gh1s = [_dot_nt(jnp.concatenate([jnp.where(first, 0, _bf(af)), jnp.where(first, 0, _bf(rf))],
                                    axis=0), kb) for af, rf, kb in zip(afs, rfs, kbs)]
    g0s = [jnp.where(strict, gh[:CHUNK], 0.0) for gh in gh0s]
    g1s = [jnp.where(strict, gh[:CHUNK], 0.0) for gh in gh1s]
    lhss = [jnp.concatenate(
        [jnp.concatenate([_bf(jnp.where(incl, ga[CHUNK:], 0.0)), z16], axis=1),
         jnp.concatenate([z16, _bf(jnp.where(incl, gb[CHUNK:], 0.0))], axis=1)], axis=0)
        for ga, gb in zip(gh0s, gh1s)]
    n_blks = [jnp.concatenate([jnp.where(first, g0, 0.0), jnp.where(first, 0.0, g1)], axis=0)
              for g0, g1 in zip(g0s, g1s)]
    akvs = [_dot(_bf(jnp.concatenate([jnp.where(first, 0.0, g0), jnp.where(first, g1, 0.0)],
                                     axis=0)), jnp.concatenate([vt, vt], axis=0))
            for g0, g1, vt in zip(g0s, g1s, vts)]
    xs = [jnp.concatenate([jnp.concatenate([af, af], axis=0), akv], axis=1)
          for af, akv in zip(afs, akvs)]
    xs = _solve_unit_lower(n_blks, xs)
    lows = [jnp.concatenate([jnp.where(first2, x[:CHUNK], x[CHUNK:]),
                             jnp.concatenate([z32, vf], axis=1)], axis=0)
            for x, vf in zip(xs, vfs)]
    lowbs = [_bf(low) for low in lows]
    outs = [_dot(lhs, jnp.concatenate([lowb, lowb[CHUNK:], lowb[:CHUNK]], axis=0))
            for lhs, lowb in zip(lhss, lowbs)]
    sels = [jnp.where(first2, out[:CHUNK], out[CHUNK:]) for out in outs]
    mcs = [_dot(_bf(low.T), bk) for low, bk in zip(lows, bks)]

    states = [s_scr[pr] for pr in range(n_pairs)]
    y_rows = []
    for c in range(n_chunks):
        y_pairs = []
        for pr in range(n_pairs):
            i = c * n_pairs + pr
            w_last = dec_in[(c + 1) * CHUNK - 1:(c + 1) * CHUNK, pr * LANES:(pr + 1) * LANES]
            g_tilde = rfs[i] + sels[i][:, :LANES]
            y_intra = sels[i][:, LANES:]
            m_lr = jnp.where(same_head, mcs[i][:LANES], 0.0) * w_last
            c_mat = jnp.where(same_head, mcs[i][LANES:], 0.0) * w_last
            s = states[pr]
            sb = _bf(s)
            y_pairs.append(_dot_nt(_bf(g_tilde), sb) + y_intra)
            states[pr] = s * w_last + _dot(sb, _bf(m_lr)) + c_mat
        y_rows.append(jnp.concatenate(y_pairs, axis=-1))
    for pr in range(n_pairs):
        s_scr[pr] = states[pr]
    y = jnp.concatenate(y_rows, axis=0)

    inv_n = 1.0 / HEAD
    mean = _dot_hilo(y, hsum) * inv_n
    dev = y - mean
    var = _dot_hilo(dev * dev, hsum) * inv_n
    yn = dev * lax.rsqrt(var + GN_EPS) * lng_ref[...] + lnb_ref[...]
    bonus = _dot_hilo(r * k_h * rk_ref[...], hsum) * v
    o_ref[0] = ((yn + bonus) * gate).astype(o_ref.dtype)


def _rwkv(pa, mu, w0, w2, a0, a2, g2, k_k, k_a, r_k, lnx_g, lnx_b, *, rows=256):
    b, t, ap = pa.shape
    rows = min(rows, t)
    width = w0.shape[0]
    w_lora, a_lora = w2.shape[0], a2.shape[0]
    assert w_lora == HEAD and a_lora == HEAD and width % LANES == 0
    n_pairs = width // LANES
    wwa = jnp.zeros((w_lora + a_lora, 2 * width), F32)
    wwa = wwa.at[:w_lora, :width].set(w2).at[w_lora:, width:].set(a2)
    ti = jnp.arange(rows)
    tri = ((ti[None, :] <= ti[:, None])
           & (ti[None, :] // CHUNK == ti[:, None] // CHUNK)).astype(BF16)
    hi = jnp.arange(width) // HEAD
    hsum = (hi[:, None] == hi[None, :]).astype(BF16)
    vec = lambda z: z.reshape(1, width)
    return pl.pallas_call(
        _rwkv_body,
        grid=(b, t // rows),
        in_specs=[
            pl.BlockSpec((1, rows, ap), lambda i, c: (i, c, 0)),
            _const_spec((1, ap)),
            _const_spec((1, width)), _const_spec((1, width)), _const_spec((1, width)),
            _const_spec((1, width)), _const_spec((1, width)), _const_spec((1, width)),
            _const_spec((1, width)),
            _const_spec((w_lora + a_lora, 2 * width)),
            _const_spec(g2.shape),
            _const_spec((rows, rows)),
            _const_spec((width, width)),
        ],
        out_specs=pl.BlockSpec((1, rows, width), lambda i, c: (i, c, 0)),
        out_shape=jax.ShapeDtypeStruct((b, t, width), BF16),
        scratch_shapes=[pltpu.VMEM((1, ap), F32), pltpu.VMEM((n_pairs, LANES, LANES), F32)],
        compiler_params=_params(2),
        name="rwkv7",
    )(pa, mu.reshape(1, ap), vec(w0), vec(a0), vec(k_k), vec(k_a), vec(r_k), vec(lnx_g),
      vec(lnx_b), _bf(wwa), _bf(g2), tri, hsum)


def _rglru_body(xb_ref, gb_ref, cw_ref, cb_ref, wg_ref, bg_ref, lam_ref, o_ref,
                prev_scr, h_scr):
    i = pl.program_id(1)
    tt, width = o_ref.shape[1], o_ref.shape[2]

    @pl.when(i == 0)
    def _():
        prev_scr[...] = jnp.zeros_like(prev_scr)
        h_scr[...] = jnp.zeros_like(h_scr)

    x = xb_ref[0]
    prev = prev_scr[...]
    prev_scr[...] = x[tt - 8:, :]
    row8 = lax.broadcasted_iota(jnp.int32, (8, width), 0)

    def delayed(d):
        rolled = pltpu.roll(x, d, axis=0)
        head = jnp.where(row8 < d, pltpu.roll(prev, d, axis=0), rolled[:8])
        return jnp.concatenate([head, rolled[8:]], axis=0)

    xc = cb_ref[...] + cw_ref[CONV_WIDTH - 1:CONV_WIDTH, :] * x
    for d in range(1, CONV_WIDTH):
        xc = xc + cw_ref[CONV_WIDTH - 1 - d:CONV_WIDTH - d, :] * delayed(d)

    gates = _dot(_bf(xc), wg_ref[...]) + bg_ref[...]
    r_g = _sigmoid(gates[:, :width])
    i_g = _sigmoid(gates[:, width:])
    log_a = -LRU_C * r_g * _softplus(-lam_ref[...])
    a = jnp.exp(log_a)
    mult = jnp.sqrt(jnp.tanh(-log_a) * (1.0 + a * a))
    u = mult * i_g * xc

    row = lax.broadcasted_iota(jnp.int32, (tt, width), 0)
    d = 1
    while d < tt:
        keep = row >= d
        a_sh = jnp.where(keep, pltpu.roll(a, d, axis=0), 1.0)
        u_sh = jnp.where(keep, pltpu.roll(u, d, axis=0), 0.0)
        u = a * u_sh + u
        a = a * a_sh
        d *= 2
    h = u + a * h_scr[...]
    h_scr[...] = h[tt - 1:, :]
    o_ref[0] = (h * _gelu_tanh(gb_ref[0])).astype(o_ref.dtype)


def _rglru(xb, gb, conv_w, conv_b, gate_a_w, gate_a_b, gate_x_w, gate_x_b, lam, *, tt=256):
    b, t, width = xb.shape
    tt = min(tt, t)
    nb, bd, _ = gate_a_w.shape
    eye = jnp.eye(nb, dtype=F32)
    dense = lambda w: jnp.einsum("nij,nm->nimj", w, eye).reshape(width, width)
    wg = _bf(jnp.concatenate([dense(gate_a_w), dense(gate_x_w)], axis=1))
    bg = jnp.concatenate([gate_a_b, gate_x_b]).reshape(1, 2 * width)
    tile = pl.BlockSpec((1, tt, width), lambda i, j: (i, j, 0))
    return pl.pallas_call(
        _rglru_body,
        grid=(b, t // tt),
        in_specs=[tile, tile, _const_spec((CONV_WIDTH, width)), _const_spec((1, width)),
                  _const_spec((width, 2 * width)), _const_spec((1, 2 * width)),
                  _const_spec((1, width))],
        out_specs=tile,
        out_shape=jax.ShapeDtypeStruct((b, t, width), BF16),
        scratch_shapes=[pltpu.VMEM((8, width), F32), pltpu.VMEM((1, width), F32)],
        compiler_params=_params(2),
        name="rglru",
    )(xb, gb, conv_w, conv_b.reshape(1, width), wg, bg, lam.reshape(1, width))


SB_DEPTH = 5
SB_ROWS = 64


def _sb_body(q_ref, k_ref, v_ref, later_ref, o_ref, zn_scr, cs_scr, lsig_scr, nsum_scr,
             negb_scr, att_scr, acc_scr, run_scr, *, blk):
    n_q = q_ref.shape[1] // blk
    n_steps = n_q * (n_q + 1) // 2 + SB_DEPTH - 1
    first = lax.broadcasted_iota(jnp.int32, (blk, LANES), 1) < HEAD
    later = later_ref[...]
    rr = lax.broadcasted_iota(jnp.int32, (2 * blk, blk), 0)
    cc = lax.broadcasted_iota(jnp.int32, (2 * blk, blk), 1)
    ahead = cc - jnp.where(rr < blk, rr, rr - blk)

    def rows(ref, i):
        return ref[0, pl.ds(pl.multiple_of(i * blk, blk), blk), :]

    def step(cur, pairs):
        old = 1 - cur
        qis, js = pairs
        qa = rows(q_ref, jnp.minimum(qis[0], n_q - 1))
        q2 = jnp.concatenate([jnp.where(first, qa, 0), jnp.where(first, 0, qa)], axis=0)
        zn_scr[cur] = _dot_nt(q2, rows(k_ref, jnp.minimum(js[0], n_q - 1)))
        cs_scr[cur] = _dot(negb_scr[old], later)
        pv = _dot(att_scr[old], rows(v_ref, js[4]))
        restart = js[3] == qis[3]
        for r in range(0, 2 * blk, SB_ROWS):
            rs = slice(r, r + SB_ROWS)
            run_in = jnp.where(restart, 0.0, run_scr[rs])
            att_scr[cur, rs] = _bf(jnp.exp2(lsig_scr[cur, rs] + cs_scr[old, rs] + run_in))
            run_scr[rs] = run_in + nsum_scr[cur, rs]
        shift = (qis[1] - js[1]) * blk
        for r in range(0, 2 * blk, SB_ROWS):
            rs = slice(r, r + SB_ROWS)
            znm = jnp.where(ahead[rs] < shift, zn_scr[old, rs], 1e30)
            low = jnp.minimum(znm, 0.0)
            top = low - znm
            lse = jnp.log(1.0 + jnp.exp2(low + top)) * LOG2E
            neg = low - lse
            negb_scr[cur, rs] = _bf(neg)
            lsig_scr[cur, rs] = top - lse
            nsum_scr[cur, rs] = jnp.sum(neg, axis=-1, keepdims=True)
        acc = jnp.where(js[4] == qis[4], 0.0, acc_scr[...]) + pv
        acc_scr[...] = acc
        o_ref[0, pl.ds(pl.multiple_of(qis[4] * blk, blk), blk), :] = jnp.where(
            first, acc[:blk], acc[blk:]).astype(o_ref.dtype)
        done = js[0] == 0
        q_next = jnp.where(done, qis[0] + 1, qis[0])
        j_next = jnp.where(done, qis[0] + 1, js[0] - 1)
        return (q_next,) + qis[:-1], (j_next,) + js[:-1]

    for scr in (zn_scr, cs_scr, lsig_scr, nsum_scr, negb_scr, att_scr, acc_scr, run_scr):
        scr[...] = jnp.zeros_like(scr)
    pairs = ((jnp.int32(0),) * SB_DEPTH,) * 2
    if n_steps % 2:
        pairs = step(1, pairs)
    lax.fori_loop(0, n_steps // 2, lambda _, pr: step(1, step(0, pr)), pairs)


def _stick_breaking(q, k, v, *, blk=256):
    b, t, width = q.shape
    blk = min(blk, t)
    n_pairs = width // LANES
    ki = jnp.arange(blk)
    later = (ki[:, None] > ki[None, :]).astype(BF16)
    seq = pl.BlockSpec((1, t, LANES), lambda i, p: (i, 0, p))
    return pl.pallas_call(
        functools.partial(_sb_body, blk=blk),
        grid=(b, n_pairs),
        in_specs=[seq, seq, seq, _const_spec((blk, blk))],
        out_specs=seq,
        out_shape=jax.ShapeDtypeStruct((b, t, width), BF16),
        scratch_shapes=[pltpu.VMEM((2, 2 * blk, blk), F32), pltpu.VMEM((2, 2 * blk, blk), F32),
                        pltpu.VMEM((2, 2 * blk, blk), F32), pltpu.VMEM((2, 2 * blk, 1), F32),
                        pltpu.VMEM((2, 2 * blk, blk), BF16), pltpu.VMEM((2, 2 * blk, blk), BF16),
                        pltpu.VMEM((2 * blk, LANES), F32), pltpu.VMEM((2 * blk, 1), F32)],
        compiler_params=_params(2),
        name="stick_breaking",
    )(q, k, v, later)


def kernel(x, l0_ffn1_pre_g, l0_ffn1_post_g, l0_ffn1_w_in, l0_ffn1_w_out, l0_mix_pre_g, l0_mix_post_g, l0_w_in, l0_mu, l0_w0, l0_w2, l0_a0, l0_a2, l0_g2, l0_k_k, l0_k_a, l0_r_k, l0_lnx_g, l0_lnx_b, l0_conv_w, l0_conv_b, l0_gate_a_w, l0_gate_a_b, l0_gate_x_w, l0_gate_x_b, l0_lambda, l0_w_out, l0_ffn2_pre_g, l0_ffn2_post_g, l0_ffn2_w_in, l0_ffn2_w_out, l1_ffn1_pre_g, l1_ffn1_post_g, l1_ffn1_w_in, l1_ffn1_w_out, l1_mix_pre_g, l1_mix_post_g, l1_w_qkv, l1_w_out, l1_ffn2_pre_g, l1_ffn2_post_g, l1_ffn2_w_in, l1_ffn2_w_out):
    b, t, d = x.shape
    a_width = l0_w0.shape[0]
    b_width = l0_lambda.shape[0]
    a_proj = l0_mu.shape[0]
    h = x.reshape(b * t, d)

    h = _ffn(h, l0_ffn1_pre_g, l0_ffn1_post_g, l0_ffn1_w_in, l0_ffn1_w_out)
    pa, xb, gb = _proj(h, l0_mix_pre_g, l0_w_in, (a_proj, b_width, b_width), F32)
    ya = _rwkv(pa.reshape(b, t, a_proj), l0_mu, l0_w0, l0_w2, l0_a0, l0_a2, l0_g2, l0_k_k,
               l0_k_a, l0_r_k.reshape(-1), l0_lnx_g, l0_lnx_b)
    yb = _rglru(xb.reshape(b, t, b_width), gb.reshape(b, t, b_width), l0_conv_w, l0_conv_b,
                l0_gate_a_w, l0_gate_a_b, l0_gate_x_w, l0_gate_x_b, l0_lambda)
    h = _outproj([ya.reshape(b * t, a_width), yb.reshape(b * t, b_width)],
                 [l0_w_out[:a_width], l0_w_out[a_width:]], h, l0_mix_post_g)
    h = _ffn(h, l0_ffn2_pre_g, l0_ffn2_post_g, l0_ffn2_w_in, l0_ffn2_w_out)

    h = _ffn(h, l1_ffn1_pre_g, l1_ffn1_post_g, l1_ffn1_w_in, l1_ffn1_w_out)
    q, k, v = _proj(h, l1_mix_pre_g, l1_w_qkv, (d, d, d), BF16,
                    scales=(-LOG2E / math.sqrt(HEAD), None, None))
    o = _stick_breaking(q.reshape(b, t, d), k.reshape(b, t, d), v.reshape(b, t, d))
    h = _outproj([o.reshape(b * t, d)], [l1_w_out], h, l1_mix_post_g)
    h = _ffn(h, l1_ffn2_pre_g, l1_ffn2_post_g, l1_ffn2_w_in, l1_ffn2_w_out)
    return h.reshape(b, t, d)
```

```python
import functools
import math

import jax
import jax.numpy as jnp
from jax import lax
from jax.experimental import pallas as pl
from jax.experimental.pallas import tpu as pltpu

F32 = jnp.float32
BF16 = jnp.bfloat16

NORM_EPS = 1e-6
GN_EPS = 64e-5
LRU_C = 8.0
HEAD = 64
LANES = 128
CHUNK = 64
CONV_WIDTH = 4
LOG2E = 1.0 / math.log(2.0)
VMEM_LIMIT = 56 * 1024 * 1024


def _params(n_axes, vmem=VMEM_LIMIT):
    return pltpu.CompilerParams(
        dimension_semantics=("arbitrary",) * n_axes, vmem_limit_bytes=vmem)


def _bf(x):
    return x.astype(BF16)


def _dot(a, b):
    return jnp.dot(a, b, preferred_element_type=F32)


def _dot_nt(a, b):
    return lax.dot_general(a, b, (((1,), (1,)), ((), ())), preferred_element_type=F32)


def _dot_hilo(x, e):
    hi = _bf(x)
    lo = _bf(x - hi.astype(F32))
    return _dot(hi, e) + _dot(lo, e)


def _rms(x, g):
    return x * lax.rsqrt(jnp.mean(x * x, axis=-1, keepdims=True) + NORM_EPS) * g


def _sigmoid(x):
    return 1.0 / (1.0 + jnp.exp(-x))


def _softplus(x):
    return jnp.maximum(x, 0.0) + jnp.log1p(jnp.exp(-jnp.abs(x)))


def _gelu_tanh(x):
    c = math.sqrt(2.0 / math.pi)
    return 0.5 * x * (1.0 + jnp.tanh(c * (x + 0.044715 * (x * x * x))))


def _row_spec(tm, n):
    return pl.BlockSpec((tm, n), lambda i, *_: (i, 0))


def _const_spec(shape):
    return pl.BlockSpec(shape, lambda *_: (0,) * len(shape))


def _ffn_body(x_ref, gpre_ref, gpost_ref, win_ref, wo_ref, o_ref, *, tf):
    f = wo_ref.shape[0]
    x = x_ref[...]
    h = _bf(_rms(x, gpre_ref[...]))

    def gate_up(c):
        return (_dot(h, win_ref[:, c * tf:(c + 1) * tf]),
                _dot(h, win_ref[:, f + c * tf:f + (c + 1) * tf]))

    n = f // tf
    acc = None
    gate, up = gate_up(0)
    for c in range(n):
        act = _bf(gate * _sigmoid(gate) * up)
        if c + 1 < n:
            gate, up = gate_up(c + 1)
        part = _dot(act, wo_ref[c * tf:(c + 1) * tf, :])
        acc = part if acc is None else acc + part
    o_ref[...] = x + 0.5 * _rms(acc, gpost_ref[...])


def _ffn(x, g_pre, g_post, w_in, w_out, *, tm=512, tf=256):
    m, d = x.shape
    tm = min(tm, m)
    f = w_out.shape[0]
    resident = lambda shape: pl.BlockSpec(shape, lambda i: (0, 0), pipeline_mode=pl.Buffered(1))
    return pl.pallas_call(
        functools.partial(_ffn_body, tf=tf),
        grid=(m // tm,),
        in_specs=[_row_spec(tm, d), _const_spec((1, d)), _const_spec((1, d)),
                  resident((d, 2 * f)), resident((f, d))],
        out_specs=_row_spec(tm, d),
        out_shape=jax.ShapeDtypeStruct((m, d), F32),
        compiler_params=_params(1),
        name="ffn",
    )(x, g_pre.reshape(1, d), g_post.reshape(1, d), _bf(w_in), _bf(w_out))


def _proj_body(x_ref, g_ref, w_ref, *o_refs, scales):
    h = _bf(_rms(x_ref[...], g_ref[...]))
    y = _dot(h, w_ref[...])
    off = 0
    for o_ref, scale in zip(o_refs, scales):
        n = o_ref.shape[-1]
        part = y[:, off:off + n]
        if scale is not None:
            part = part * scale
        o_ref[...] = part.astype(o_ref.dtype)
        off += n


def _proj(x, g, w, splits, dtype, *, scales=None, tm=512):
    m, d = x.shape
    tm = min(tm, m)
    n = w.shape[1]
    assert sum(splits) == n
    scales = scales or (None,) * len(splits)
    return pl.pallas_call(
        functools.partial(_proj_body, scales=scales),
        grid=(m // tm,),
        in_specs=[_row_spec(tm, d), _const_spec((1, d)), _const_spec((d, n))],
        out_specs=[_row_spec(tm, s) for s in splits],
        out_shape=[jax.ShapeDtypeStruct((m, s), dtype) for s in splits],
        compiler_params=_params(1),
        name="proj",
    )(x, g.reshape(1, d), _bf(w))


def _proj_tiles_body(x_ref, g_ref, w_ref, o_ref, *, scales):
    h = _bf(_rms(x_ref[...], g_ref[...]))
    y = _dot(h, w_ref[...])
    n_groups, n_tiles = o_ref.shape[0], o_ref.shape[1]
    for s in range(n_groups):
        for p in range(n_tiles):
            off = (s * n_tiles + p) * LANES
            part = y[:, off:off + LANES]
            if scales[s] is not None:
                part = part * scales[s]
            o_ref[s, p] = part.astype(o_ref.dtype)


def _proj_tiles(x, g, w, n_groups, dtype, *, scales, tm=512):
    m, d = x.shape
    tm = min(tm, m)
    n = w.shape[1]
    n_tiles = n // (n_groups * LANES)
    return pl.pallas_call(
        functools.partial(_proj_tiles_body, scales=scales),
        grid=(m // tm,),
        in_specs=[_row_spec(tm, d), _const_spec((1, d)), _const_spec((d, n))],
        out_specs=pl.BlockSpec((n_groups, n_tiles, tm, LANES), lambda i: (0, 0, i, 0)),
        out_shape=jax.ShapeDtypeStruct((n_groups, n_tiles, m, LANES), dtype),
        compiler_params=_params(1),
        name="proj_tiles",
    )(x, g.reshape(1, d), _bf(w))


def _outproj_body(*refs, n_in):
    a_refs = refs[:n_in]
    w_refs = refs[n_in:2 * n_in]
    res_ref, g_ref, o_ref = refs[2 * n_in:]

    def act(a_ref):
        if len(a_ref.shape) == 2:
            return a_ref[...]
        return jnp.concatenate([a_ref[p] for p in range(a_ref.shape[0])], axis=1)

    y = _dot(act(a_refs[0]), w_refs[0][...])
    for a_ref, w_ref in zip(a_refs[1:], w_refs[1:]):
        y += _dot(act(a_ref), w_ref[...])
    o_ref[...] = res_ref[...] + _rms(y, g_ref[...])


def _outproj(acts, ws, res, g, *, tm=512):
    m, d = res.shape
    tm = min(tm, m)
    n_in = len(acts)
    act_spec = lambda a: (_row_spec(tm, a.shape[1]) if a.ndim == 2 else
                          pl.BlockSpec((a.shape[0], tm, a.shape[2]), lambda i: (0, i, 0)))
    return pl.pallas_call(
        functools.partial(_outproj_body, n_in=n_in),
        grid=(m // tm,),
        in_specs=([act_spec(a) for a in acts]
                  + [_const_spec(w.shape) for w in ws]
                  + [_row_spec(tm, d), _const_spec((1, d))]),
        out_specs=_row_spec(tm, d),
        out_shape=jax.ShapeDtypeStruct((m, d), F32),
        compiler_params=_params(1),
        name="outproj",
    )(*acts, *[_bf(w) for w in ws], res, g.reshape(1, d))


def _solve_unit_lower(ns, xs):
    ps = [_bf(n) for n in ns]
    steps = int(math.log2(CHUNK))
    for i in range(steps):
        xs = [x + _dot(p, _bf(x)) for p, x in zip(ps, xs)]
        if i + 1 < steps:
            ps = [_bf(_dot(p, p)) for p in ps]
    return xs


def _rwkv_body(pa_ref, mu_ref, w0_ref, a0_ref, kk_ref, ka_ref, rk_ref, lng_ref, lnb_ref,
               wwa_ref, g2_ref, tri_ref, hsum_ref, o_ref, prev_scr, s_scr):
    step = pl.program_id(1)
    rows, width = o_ref.shape[1], o_ref.shape[2]
    n_pairs = width // LANES
    n_chunks = rows // CHUNK

    @pl.when(step == 0)
    def _():
        prev_scr[...] = jnp.zeros_like(prev_scr)
        s_scr[...] = jnp.zeros_like(s_scr)

    p = pa_ref[0]
    row = lax.broadcasted_iota(jnp.int32, p.shape, 0)
    shifted = jnp.where(row == 0, prev_scr[...], pltpu.roll(p, 1, axis=0))
    prev_scr[...] = p[rows - 1:rows, :]
    xm = p + mu_ref[...] * (shifted - p)

    r = xm[:, 0:width]
    k = xm[:, width:2 * width]
    v = xm[:, 2 * width:3 * width]
    wa_lo = xm[:, 3 * width:3 * width + LANES]
    g_lo = xm[:, 3 * width + LANES:]
    lane_wa = lax.broadcasted_iota(jnp.int32, wa_lo.shape, 1)
    wa_act = jnp.where(lane_wa < LANES // 2, jnp.tanh(wa_lo), wa_lo)
    lora = _dot(_bf(wa_act), wwa_ref[...])
    w_log = -_softplus(-(w0_ref[...] + lora[:, :width])) - 0.5
    log_decay = -jnp.exp(w_log)
    a = _sigmoid(a0_ref[...] + lora[:, width:])
    gate = _dot(_bf(_sigmoid(g_lo)), g2_ref[...])

    hsum = hsum_ref[...]
    kk = k * kk_ref[...]
    kk = kk / jnp.maximum(jnp.sqrt(_dot_hilo(kk * kk, hsum)), 1e-12)
    k_h = k * (1.0 + (a - 1.0) * ka_ref[...])

    ld_hi = _bf(log_decay)
    ld_lo = _bf(log_decay - ld_hi.astype(F32))
    tri = tri_ref[...]
    cum = _dot(tri, ld_hi) + _dot(tri, ld_lo)
    dec_in = jnp.exp(cum)
    dec_out = jnp.exp(-cum)
    r_f = r * dec_in
    a_f = -kk * jnp.exp(cum - log_decay)
    b_t = _bf(kk * a * dec_out)
    k_t = _bf(k_h * dec_out)

    lane = lax.broadcasted_iota(jnp.int32, (CHUNK, LANES), 1)
    trow = lax.broadcasted_iota(jnp.int32, (CHUNK, LANES), 0)
    first = lane < HEAD
    src = jnp.where(first, lane, lane - HEAD)
    strict = src < trow
    incl = src <= trow
    lane2 = lax.broadcasted_iota(jnp.int32, (CHUNK, 2 * LANES), 1)
    first2 = jnp.where(lane2 < LANES, lane2, lane2 - LANES) < HEAD
    sr = lax.broadcasted_iota(jnp.int32, (LANES, LANES), 0)
    sc = lax.broadcasted_iota(jnp.int32, (LANES, LANES), 1)
    same_head = (sr < HEAD) == (sc < HEAD)
    z16 = jnp.zeros((CHUNK, LANES), BF16)
    z32 = jnp.zeros((CHUNK, LANES), F32)

    inst = [(c, pr) for c in range(n_chunks) for pr in range(n_pairs)]
    blk = lambda z, c, pr: z[c * CHUNK:(c + 1) * CHUNK, pr * LANES:(pr + 1) * LANES]
    afs = [blk(a_f, c, pr) for c, pr in inst]
    rfs = [blk(r_f, c, pr) for c, pr in inst]
    vfs = [blk(v, c, pr) for c, pr in inst]
    vts = [_bf(vf) for vf in vfs]
    bks = [jnp.concatenate([blk(b_t, c, pr), blk(k_t, c, pr)], axis=0) for c, pr in inst]
    kbs = [jnp.concatenate([blk(k_t, c, pr), blk(b_t, c, pr)], axis=0) for c, pr in inst]
    gh0s = [_dot_nt(jnp.concatenate([jnp.where(first, _bf(af), 0), jnp.where(first, _bf(rf), 0)],
                                    axis=0), bk) for af, rf, bk in zip(afs, rfs, bks)]
    gh1s = [_dot_nt(jnp.concatenate([jnp.where(first, 0, _bf(af)), jnp.where(first, 0, _bf(rf))],
                                    axis=0), kb) for af, rf, kb in zip(afs, rfs, kbs)]
    g0s = [jnp.where(strict, gh[:CHUNK], 0.0) for gh in gh0s]
    g1s = [jnp.where(strict, gh[:CHUNK], 0.0) for gh in gh1s]
    lhss = [jnp.concatenate(
        [jnp.concatenate([_bf(jnp.where(incl, ga[CHUNK:], 0.0)), z16], axis=1),
         jnp.concatenate([z16, _bf(jnp.where(incl, gb[CHUNK:], 0.0))], axis=1)], axis=0)
        for ga, gb in zip(gh0s, gh1s)]
    n_blks = [jnp.concatenate([jnp.where(first, g0, 0.0), jnp.where(first, 0.0, g1)], axis=0)
              for g0, g1 in zip(g0s, g1s)]
    akvs = [_dot(_bf(jnp.concatenate([jnp.where(first, 0.0, g0), jnp.where(first, g1, 0.0)],
                                     axis=0)), jnp.concatenate([vt, vt], axis=0))
            for g0, g1, vt in zip(g0s, g1s, vts)]
    xs = [jnp.concatenate([jnp.concatenate([af, af], axis=0), akv], axis=1)
          for af, akv in zip(afs, akvs)]
    xs = _solve_unit_lower(n_blks, xs)
    lows = [jnp.concatenate([jnp.where(first2, x[:CHUNK], x[CHUNK:]),
                             jnp.concatenate([z32, vf], axis=1)], axis=0)
            for x, vf in zip(xs, vfs)]
    lowbs = [_bf(low) for low in lows]
    outs = [_dot(lhs, jnp.concatenate([lowb, lowb[CHUNK:], lowb[:CHUNK]], axis=0))
            for lhs, lowb in zip(lhss, lowbs)]
    sels = [jnp.where(first2, out[:CHUNK], out[CHUNK:]) for out in outs]
    mcs = [_dot(_bf(low.T), bk) for low, bk in zip(lows, bks)]

    states = [s_scr[pr] for pr in range(n_pairs)]
    y_rows = []
    for c in range(n_chunks):
        y_pairs = []
        for pr in range(n_pairs):
            i = c * n_pairs + pr
            w_last = dec_in[(c + 1) * CHUNK - 1:(c + 1) * CHUNK, pr * LANES:(pr + 1) * LANES]
            g_tilde = rfs[i] + sels[i][:, :LANES]
            y_intra = sels[i][:, LANES:]
            m_lr = jnp.where(same_head, mcs[i][:LANES], 0.0) * w_last
            c_mat = jnp.where(same_head, mcs[i][LANES:], 0.0) * w_last
            s = states[pr]
            sb = _bf(s)
            y_pairs.append(_dot_nt(_bf(g_tilde), sb) + y_intra)
            states[pr] = s * w_last + _dot(sb, _bf(m_lr)) + c_mat
        y_rows.append(jnp.concatenate(y_pairs, axis=-1))
    for pr in range(n_pairs):
        s_scr[pr] = states[pr]
    y = jnp.concatenate(y_rows, axis=0)

    inv_n = 1.0 / HEAD
    mean = _dot_hilo(y, hsum) * inv_n
    dev = y - mean
    var = _dot_hilo(dev * dev, hsum) * inv_n
    yn = dev * lax.rsqrt(var + GN_EPS) * lng_ref[...] + lnb_ref[...]
    bonus = _dot_hilo(r * k_h * rk_ref[...], hsum) * v
    o_ref[0] = ((yn + bonus) * gate).astype(o_ref.dtype)


def _rwkv(pa, mu, w0, w2, a0, a2, g2, k_k, k_a, r_k, lnx_g, lnx_b, *, rows=256):
    b, t, ap = pa.shape
    rows = min(rows, t)
    width = w0.shape[0]
    w_lora, a_lora = w2.shape[0], a2.shape[0]
    assert w_lora == HEAD and a_lora == HEAD and width % LANES == 0
    n_pairs = width // LANES
    wwa = jnp.zeros((w_lora + a_lora, 2 * width), F32)
    wwa = wwa.at[:w_lora, :width].set(w2).at[w_lora:, width:].set(a2)
    ti = jnp.arange(rows)
    tri = ((ti[None, :] <= ti[:, None])
           & (ti[None, :] // CHUNK == ti[:, None] // CHUNK)).astype(BF16)
    hi = jnp.arange(width) // HEAD
    hsum = (hi[:, None] == hi[None, :]).astype(BF16)
    vec = lambda z: z.reshape(1, width)
    return pl.pallas_call(
        _rwkv_body,
        grid=(b, t // rows),
        in_specs=[
            pl.BlockSpec((1, rows, ap), lambda i, c: (i, c, 0)),
            _const_spec((1, ap)),
            _const_spec((1, width)), _const_spec((1, width)), _const_spec((1, width)),
            _const_spec((1, width)), _const_spec((1, width)), _const_spec((1, width)),
            _const_spec((1, width)),
            _const_spec((w_lora + a_lora, 2 * width)),
            _const_spec(g2.shape),
            _const_spec((rows, rows)),
            _const_spec((width, width)),
        ],
        out_specs=pl.BlockSpec((1, rows, width), lambda i, c: (i, c, 0)),
        out_shape=jax.ShapeDtypeStruct((b, t, width), BF16),
        scratch_shapes=[pltpu.VMEM((1, ap), F32), pltpu.VMEM((n_pairs, LANES, LANES), F32)],
        compiler_params=_params(2),
        name="rwkv7",
    )(pa, mu.reshape(1, ap), vec(w0), vec(a0), vec(k_k), vec(k_a), vec(r_k), vec(lnx_g),
      vec(lnx_b), _bf(wwa), _bf(g2), tri, hsum)


def _rglru_body(xb_ref, gb_ref, cw_ref, cb_ref, wg_ref, bg_ref, lam_ref, o_ref,
                prev_scr, h_scr):
    i = pl.program_id(1)
    tt, width = o_ref.shape[1], o_ref.shape[2]

    @pl.when(i == 0)
    def _():
        prev_scr[...] = jnp.zeros_like(prev_scr)
        h_scr[...] = jnp.zeros_like(h_scr)

    x = xb_ref[0]
    prev = prev_scr[...]
    prev_scr[...] = x[tt - 8:, :]
    row8 = lax.broadcasted_iota(jnp.int32, (8, width), 0)

    def delayed(d):
        rolled = pltpu.roll(x, d, axis=0)
        head = jnp.where(row8 < d, pltpu.roll(prev, d, axis=0), rolled[:8])
        return jnp.concatenate([head, rolled[8:]], axis=0)

    xc = cb_ref[...] + cw_ref[CONV_WIDTH - 1:CONV_WIDTH, :] * x
    for d in range(1, CONV_WIDTH):
        xc = xc + cw_ref[CONV_WIDTH - 1 - d:CONV_WIDTH - d, :] * delayed(d)

    gates = _dot(_bf(xc), wg_ref[...]) + bg_ref[...]
    r_g = _sigmoid(gates[:, :width])
    i_g = _sigmoid(gates[:, width:])
    log_a = -LRU_C * r_g * _softplus(-lam_ref[...])
    a = jnp.exp(log_a)
    mult = jnp.sqrt(jnp.tanh(-log_a) * (1.0 + a * a))
    u = mult * i_g * xc

    row = lax.broadcasted_iota(jnp.int32, (tt, width), 0)
    d = 1
    while d < tt:
        keep = row >= d
        a_sh = jnp.where(keep, pltpu.roll(a, d, axis=0), 1.0)
        u_sh = jnp.where(keep, pltpu.roll(u, d, axis=0), 0.0)
        u = a * u_sh + u
        a = a * a_sh
        d *= 2
    h = u + a * h_scr[...]
    h_scr[...] = h[tt - 1:, :]
    o_ref[0] = (h * _gelu_tanh(gb_ref[0])).astype(o_ref.dtype)


def _rglru(xb, gb, conv_w, conv_b, gate_a_w, gate_a_b, gate_x_w, gate_x_b, lam, *, tt=256):
    b, t, width = xb.shape
    tt = min(tt, t)
    nb, bd, _ = gate_a_w.shape
    eye = jnp.eye(nb, dtype=F32)
    dense = lambda w: jnp.einsum("nij,nm->nimj", w, eye).reshape(width, width)
    wg = _bf(jnp.concatenate([dense(gate_a_w), dense(gate_x_w)], axis=1))
    bg = jnp.concatenate([gate_a_b, gate_x_b]).reshape(1, 2 * width)
    tile = pl.BlockSpec((1, tt, width), lambda i, j: (i, j, 0))
    return pl.pallas_call(
        _rglru_body,
        grid=(b, t // tt),
        in_specs=[tile, tile, _const_spec((CONV_WIDTH, width)), _const_spec((1, width)),
                  _const_spec((width, 2 * width)), _const_spec((1, 2 * width)),
                  _const_spec((1, width))],
        out_specs=tile,
        out_shape=jax.ShapeDtypeStruct((b, t, width), BF16),
        scratch_shapes=[pltpu.VMEM((8, width), F32), pltpu.VMEM((1, width), F32)],
        compiler_params=_params(2),
        name="rglru",
    )(xb, gb, conv_w, conv_b.reshape(1, width), wg, bg, lam.reshape(1, width))


SB_DEPTH = 5
SB_ROWS = 64


def _sb_body(qkv_ref, later_ref, o_ref, zn_scr, cs_scr, lsig_scr, negb_scr, att_scr, acc_scr,
             run_scr, *, blk):
    n_p = qkv_ref.shape[1]
    n_q = qkv_ref.shape[2] // blk
    n_steps = n_p * n_q * (n_q + 1) // 2 + SB_DEPTH - 1
    first = lax.broadcasted_iota(jnp.int32, (blk, LANES), 1) < HEAD
    later = later_ref[...]
    rr = lax.broadcasted_iota(jnp.int32, (2 * blk, blk), 0)
    cc = lax.broadcasted_iota(jnp.int32, (2 * blk, blk), 1)
    ahead = cc - jnp.where(rr < blk, rr, rr - blk)

    def rows(which, p, i):
        return qkv_ref[which, p, pl.ds(pl.multiple_of(i * blk, blk), blk), :]

    def step(cur, triples):
        old = 1 - cur
        ps, qis, js = triples
        p0 = jnp.minimum(ps[0], n_p - 1)
        qa = rows(0, p0, jnp.minimum(qis[0], n_q - 1))
        q2 = jnp.concatenate([jnp.where(first, qa, 0), jnp.where(first, 0, qa)], axis=0)
        zn_scr[cur] = _dot_nt(q2, rows(1, p0, jnp.minimum(js[0], n_q - 1)))
        cs_scr[cur] = _dot(negb_scr[old], later)
        pv = _dot(att_scr[old], rows(2, ps[4], js[4]))
        for r in range(0, 2 * blk, SB_ROWS):
            rs = slice(r, r + SB_ROWS)
            att_scr[cur, rs] = _bf(jnp.exp2(lsig_scr[cur, rs] + cs_scr[old, rs]))
        for r in range(0, 2 * blk, SB_ROWS):
            rs = slice(r, r + SB_ROWS)
            znm = jnp.where(ahead[rs] < (qis[1] - js[1]) * blk, zn_scr[old, rs], 1e30)
            low = jnp.minimum(znm, 0.0)
            top = low - znm
            lse = jnp.log(1.0 + jnp.exp2(low + top)) * LOG2E
            neg = low - lse
            run = jnp.where(js[1] == qis[1], 0.0, run_scr[rs])
            negb_scr[cur, rs] = _bf(neg)
            lsig_scr[cur, rs] = (top - lse) + run
            run_scr[rs] = run + jnp.sum(neg, axis=-1, keepdims=True)
        acc = jnp.where(js[4] == qis[4], 0.0, acc_scr[...]) + pv
        acc_scr[...] = acc
        o_ref[ps[4], pl.ds(pl.multiple_of(qis[4] * blk, blk), blk), :] = jnp.where(
            first, acc[:blk], acc[blk:]).astype(o_ref.dtype)
        blk_done = js[0] == 0
        pair_done = jnp.logical_and(blk_done, qis[0] == n_q - 1)
        q_new = jnp.where(pair_done, 0, qis[0] + 1)
        p_next = jnp.where(pair_done, ps[0] + 1, ps[0])
        q_next = jnp.where(blk_done, q_new, qis[0])
        j_next = jnp.where(blk_done, q_new, js[0] - 1)
        return (p_next,) + ps[:-1], (q_next,) + qis[:-1], (j_next,) + js[:-1]

    for scr in (zn_scr, cs_scr, lsig_scr, negb_scr, att_scr, acc_scr, run_scr):
        scr[...] = jnp.zeros_like(scr)
    triples = ((jnp.int32(0),) * SB_DEPTH,) * 3
    if n_steps % 2:
        triples = step(1, triples)
    lax.fori_loop(0, n_steps // 2, lambda _, tr: step(1, step(0, tr)), triples)


def _stick_breaking(qkv, batch, *, blk=256):
    _, n_p, m, _ = qkv.shape
    t = m // batch
    blk = min(blk, t)
    ki = jnp.arange(blk)
    later = (ki[:, None] > ki[None, :]).astype(BF16)
    wide = lambda dt: pltpu.VMEM((2, 2 * blk, blk), dt)
    return pl.pallas_call(
        functools.partial(_sb_body, blk=blk),
        grid=(batch,),
        in_specs=[pl.BlockSpec((3, n_p, t, LANES), lambda i: (0, 0, i, 0)),
                  _const_spec((blk, blk))],
        out_specs=pl.BlockSpec((n_p, t, LANES), lambda i: (0, i, 0)),
        out_shape=jax.ShapeDtypeStruct((n_p, m, LANES), BF16),
        scratch_shapes=[wide(F32), wide(F32), wide(F32), wide(BF16), wide(BF16),
                        pltpu.VMEM((2 * blk, LANES), F32), pltpu.VMEM((2 * blk, 1), F32)],
        compiler_params=_params(1),
        name="stick_breaking",
    )(qkv, later)


def kernel(x, l0_ffn1_pre_g, l0_ffn1_post_g, l0_ffn1_w_in, l0_ffn1_w_out, l0_mix_pre_g, l0_mix_post_g, l0_w_in, l0_mu, l0_w0, l0_w2, l0_a0, l0_a2, l0_g2, l0_k_k, l0_k_a, l0_r_k, l0_lnx_g, l0_lnx_b, l0_conv_w, l0_conv_b, l0_gate_a_w, l0_gate_a_b, l0_gate_x_w, l0_gate_x_b, l0_lambda, l0_w_out, l0_ffn2_pre_g, l0_ffn2_post_g, l0_ffn2_w_in, l0_ffn2_w_out, l1_ffn1_pre_g, l1_ffn1_post_g, l1_ffn1_w_in, l1_ffn1_w_out, l1_mix_pre_g, l1_mix_post_g, l1_w_qkv, l1_w_out, l1_ffn2_pre_g, l1_ffn2_post_g, l1_ffn2_w_in, l1_ffn2_w_out):
    b, t, d = x.shape
    a_width = l0_w0.shape[0]
    b_width = l0_lambda.shape[0]
    a_proj = l0_mu.shape[0]
    h = x.reshape(b * t, d)

    h = _ffn(h, l0_ffn1_pre_g, l0_ffn1_post_g, l0_ffn1_w_in, l0_ffn1_w_out)
    pa, xb, gb = _proj(h, l0_mix_pre_g, l0_w_in, (a_proj, b_width, b_width), F32)
    ya = _rwkv(pa.reshape(b, t, a_proj), l0_mu, l0_w0, l0_w2, l0_a0, l0_a2, l0_g2, l0_k_k,
               l0_k_a, l0_r_k.reshape(-1), l0_lnx_g, l0_lnx_b)
    yb = _rglru(xb.reshape(b, t, b_width), gb.reshape(b, t, b_width), l0_conv_w, l0_conv_b,
                l0_gate_a_w, l0_gate_a_b, l0_gate_x_w, l0_gate_x_b, l0_lambda)
    h = _outproj([ya.reshape(b * t, a_width), yb.reshape(b * t, b_width)],
                 [l0_w_out[:a_width], l0_w_out[a_width:]], h, l0_mix_post_g)
    h = _ffn(h, l0_ffn2_pre_g, l0_ffn2_post_g, l0_ffn2_w_in, l0_ffn2_w_out)

    h = _ffn(h, l1_ffn1_pre_g, l1_ffn1_post_g, l1_ffn1_w_in, l1_ffn1_w_out)
    qkv = _proj_tiles(h, l1_mix_pre_g, l1_w_qkv, 3, BF16,
                      scales=(-LOG2E / math.sqrt(HEAD), None, None))
    o = _stick_breaking(qkv, b)
    h = _outproj([o], [l1_w_out], h, l1_mix_post_g)
    h = _ffn(h, l1_ffn2_pre_g, l1_ffn2_post_g, l1_ffn2_w_in, l1_ffn2_w_out)
    return h.reshape(b, t, d)
```

```python
import functools
import math

import jax
import jax.numpy as jnp
from jax import lax
from jax.experimental import pallas as pl
from jax.experimental.pallas import tpu as pltpu

F32 = jnp.float32
BF16 = jnp.bfloat16

NORM_EPS = 1e-6
GN_EPS = 64e-5
LRU_C = 8.0
HEAD = 64
LANES = 128
SUBLANES = 8
CHUNK = 64
RW_ROWS = 256
CONV_WIDTH = 4
LOG2E = 1.0 / math.log(2.0)
VMEM_LIMIT = 56 * 1024 * 1024


def _params(n_axes, vmem=VMEM_LIMIT):
    return pltpu.CompilerParams(
        dimension_semantics=("arbitrary",) * n_axes, vmem_limit_bytes=vmem)


def _bf(x):
    return x.astype(BF16)


def _dot(a, b):
    return jnp.dot(a, b, preferred_element_type=F32)


def _dot_nt(a, b):
    return lax.dot_general(a, b, (((1,), (1,)), ((), ())), preferred_element_type=F32)


def _dot_hilo(x, e):
    hi = _bf(x)
    lo = _bf(x - hi.astype(F32))
    return _dot(hi, e) + _dot(lo, e)


def _rms(x, g):
    return x * lax.rsqrt(jnp.mean(x * x, axis=-1, keepdims=True) + NORM_EPS) * g


def _sigmoid(x):
    return 1.0 / (1.0 + jnp.exp(-x))


def _softplus(x):
    return jnp.maximum(x, 0.0) + jnp.log1p(jnp.exp(-jnp.abs(x)))


def _gelu_tanh(x):
    c = math.sqrt(2.0 / math.pi)
    return 0.5 * x * (1.0 + jnp.tanh(c * (x + 0.044715 * (x * x * x))))


def _row_spec(tm, n):
    return pl.BlockSpec((tm, n), lambda i, *_: (i, 0))


def _const_spec(shape):
    return pl.BlockSpec(shape, lambda *_: (0,) * len(shape))


def _act_rows(a_ref):
    if len(a_ref.shape) == 2:
        return a_ref[...]
    return jnp.concatenate([a_ref[p] for p in range(a_ref.shape[0])], axis=1)


def _ffn_body(*refs, tf, n_mix):
    a_refs, w_refs = refs[:n_mix], refs[n_mix:2 * n_mix]
    rest = refs[2 * n_mix:]
    if n_mix:
        gmix_ref, rest = rest[0], rest[1:]
    x_ref, gpre_ref, gpost_ref, win_ref, wo_ref, o_ref = rest
    f = wo_ref.shape[0]
    x = x_ref[...]
    if n_mix:
        y = _dot(_act_rows(a_refs[0]), w_refs[0][...])
        for a_ref, w_ref in zip(a_refs[1:], w_refs[1:]):
            y += _dot(_act_rows(a_ref), w_ref[...])
        x = x + _rms(y, gmix_ref[...])
    h = _bf(_rms(x, gpre_ref[...]))

    def gate_up(c):
        return (_dot(h, win_ref[:, c * tf:(c + 1) * tf]),
                _dot(h, win_ref[:, f + c * tf:f + (c + 1) * tf]))

    n = f // tf
    acc = None
    gate, up = gate_up(0)
    for c in range(n):
        act = _bf(gate * _sigmoid(gate) * up)
        if c + 1 < n:
            gate, up = gate_up(c + 1)
        part = _dot(act, wo_ref[c * tf:(c + 1) * tf, :])
        acc = part if acc is None else acc + part
    o_ref[...] = x + 0.5 * _rms(acc, gpost_ref[...])


def _ffn(x, g_pre, g_post, w_in, w_out, *, mix=None, tm=512, tf=256):
    m, d = x.shape
    tm = min(tm, m)
    f = w_out.shape[0]
    acts, ws, g_mix = mix if mix is not None else ((), (), None)
    resident = lambda shape: pl.BlockSpec(shape, lambda i: (0, 0), pipeline_mode=pl.Buffered(1))
    act_spec = lambda a: (_row_spec(tm, a.shape[1]) if a.ndim == 2 else
                          pl.BlockSpec((a.shape[0], tm, a.shape[2]), lambda i: (0, i, 0)))
    mix_specs = [act_spec(a) for a in acts] + [resident(w.shape) for w in ws]
    mix_args = list(acts) + [_bf(w) for w in ws]
    if mix is not None:
        mix_specs.append(_const_spec((1, d)))
        mix_args.append(g_mix.reshape(1, d))
    return pl.pallas_call(
        functools.partial(_ffn_body, tf=tf, n_mix=len(acts)),
        grid=(m // tm,),
        in_specs=mix_specs + [_row_spec(tm, d), _const_spec((1, d)), _const_spec((1, d)),
                              resident((d, 2 * f)), resident((f, d))],
        out_specs=_row_spec(tm, d),
        out_shape=jax.ShapeDtypeStruct((m, d), F32),
        compiler_params=_params(1),
        name="ffn",
    )(*mix_args, x, g_pre.reshape(1, d), g_post.reshape(1, d), _bf(w_in), _bf(w_out))


def _proj_body(x_ref, g_ref, w_ref, *o_refs, scales):
    h = _bf(_rms(x_ref[...], g_ref[...]))
    y = _dot(h, w_ref[...])
    off = 0
    for o_ref, scale in zip(o_refs, scales):
        n = o_ref.shape[-1]
        part = y[:, off:off + n]
        if scale is not None:
            part = part * scale
        o_ref[...] = part.astype(o_ref.dtype)
        off += n


def _proj(x, g, w, splits, dtype, *, scales=None, tm=512):
    m, d = x.shape
    tm = min(tm, m)
    n = w.shape[1]
    assert sum(splits) == n
    scales = scales or (None,) * len(splits)
    return pl.pallas_call(
        functools.partial(_proj_body, scales=scales),
        grid=(m // tm,),
        in_specs=[_row_spec(tm, d), _const_spec((1, d)), _const_spec((d, n))],
        out_specs=[_row_spec(tm, s) for s in splits],
        out_shape=[jax.ShapeDtypeStruct((m, s), dtype) for s in splits],
        compiler_params=_params(1),
        name="proj",
    )(x, g.reshape(1, d), _bf(w))


def _proj_tiles_body(x_ref, g_ref, w_ref, o_ref, *, scales):
    h = _bf(_rms(x_ref[...], g_ref[...]))
    y = _dot(h, w_ref[...])
    n_groups, n_tiles = o_ref.shape[0], o_ref.shape[1]
    for s in range(n_groups):
        for p in range(n_tiles):
            off = (s * n_tiles + p) * LANES
            part = y[:, off:off + LANES]
            if scales[s] is not None:
                part = part * scales[s]
            o_ref[s, p] = part.astype(o_ref.dtype)


def _proj_tiles(x, g, w, n_groups, dtype, *, scales, tm=512):
    m, d = x.shape
    tm = min(tm, m)
    n = w.shape[1]
    n_tiles = n // (n_groups * LANES)
    return pl.pallas_call(
        functools.partial(_proj_tiles_body, scales=scales),
        grid=(m // tm,),
        in_specs=[_row_spec(tm, d), _const_spec((1, d)), _const_spec((d, n))],
        out_specs=pl.BlockSpec((n_groups, n_tiles, tm, LANES), lambda i: (0, 0, i, 0)),
        out_shape=jax.ShapeDtypeStruct((n_groups, n_tiles, m, LANES), dtype),
        compiler_params=_params(1),
        name="proj_tiles",
    )(x, g.reshape(1, d), _bf(w))


def _solve_unit_lower(ns, xs):
    ps = [_bf(n) for n in ns]
    steps = int(math.log2(CHUNK))
    for i in range(steps):
        xs = [x + _dot(p, _bf(x)) for p, x in zip(ps, xs)]
        if i + 1 < steps:
            ps = [_bf(_dot(p, p)) for p in ps]
    return xs


def _rwkv_body(pa_ref, mu_ref, w0_ref, a0_ref, kk_ref, ka_ref, rk_ref, lng_ref, lnb_ref,
               wwa_ref, g2_ref, tri_ref, hsum_ref, o_ref, prev_scr, s_scr):
    @pl.when(pl.program_id(1) == 0)
    def _():
        prev_scr[...] = jnp.zeros_like(prev_scr)
        s_scr[...] = jnp.zeros_like(s_scr)

    for r0 in range(0, o_ref.shape[1], RW_ROWS):
        _rwkv_rows(r0, pa_ref, mu_ref, w0_ref, a0_ref, kk_ref, ka_ref, rk_ref, lng_ref, lnb_ref,
                   wwa_ref, g2_ref, tri_ref, hsum_ref, o_ref, prev_scr, s_scr)


def _rwkv_rows(r0, pa_ref, mu_ref, w0_ref, a0_ref, kk_ref, ka_ref, rk_ref, lng_ref, lnb_ref,
               wwa_ref, g2_ref, tri_ref, hsum_ref, o_ref, prev_scr, s_scr):
    rows, width = RW_ROWS, o_ref.shape[2]
    n_pairs = width // LANES
    n_chunks = rows // CHUNK
    p = pa_ref[0, r0:r0 + rows]
    row = lax.broadcasted_iota(jnp.int32, p.shape, 0)
    shifted = jnp.where(row == 0, prev_scr[...], pltpu.roll(p, 1, axis=0))
    prev_scr[...] = p[rows - 1:rows, :]
    xm = p + mu_ref[...] * (shifted - p)

    r = xm[:, 0:width]
    k = xm[:, width:2 * width]
    v = xm[:, 2 * width:3 * width]
    wa_lo = xm[:, 3 * width:3 * width + LANES]
    g_lo = xm[:, 3 * width + LANES:]
    lane_wa = lax.broadcasted_iota(jnp.int32, wa_lo.shape, 1)
    wa_act = jnp.where(lane_wa < LANES // 2, jnp.tanh(wa_lo), wa_lo)
    lora = _dot(_bf(wa_act), wwa_ref[...])
    w_log = -_softplus(-(w0_ref[...] + lora[:, :width])) - 0.5
    log_decay = -jnp.exp(w_log)
    a = _sigmoid(a0_ref[...] + lora[:, width:])
    gate = _dot(_bf(_sigmoid(g_lo)), g2_ref[...])

    hsum = hsum_ref[...]
    kk = k * kk_ref[...]
    kk = kk / jnp.maximum(jnp.sqrt(_dot_hilo(kk * kk, hsum)), 1e-12)
    k_h = k * (1.0 + (a - 1.0) * ka_ref[...])

    ld_hi = _bf(log_decay)
    ld_lo = _bf(log_decay - ld_hi.astype(F32))
    tri = tri_ref[...]
    cum = _dot(tri, ld_hi) + _dot(tri, ld_lo)
    dec_in = jnp.exp(cum)
    dec_out = jnp.exp(-cum)
    r_f = r * dec_in
    a_f = -kk * jnp.exp(cum - log_decay)
    b_t = _bf(kk * a * dec_out)
    k_t = _bf(k_h * dec_out)

    lane = lax.broadcasted_iota(jnp.int32, (CHUNK, LANES), 1)
    trow = lax.broadcasted_iota(jnp.int32, (CHUNK, LANES), 0)
    first = lane < HEAD
    src = jnp.where(first, lane, lane - HEAD)
    strict = src < trow
    incl = src <= trow
    lane2 = lax.broadcasted_iota(jnp.int32, (CHUNK, 2 * LANES), 1)
    first2 = jnp.where(lane2 < LANES, lane2, lane2 - LANES) < HEAD
    sr = lax.broadcasted_iota(jnp.int32, (LANES, LANES), 0)
    sc = lax.broadcasted_iota(jnp.int32, (LANES, LANES), 1)
    same_head = (sr < HEAD) == (sc < HEAD)
    z16 = jnp.zeros((CHUNK, LANES), BF16)
    z32 = jnp.zeros((CHUNK, LANES), F32)

    inst = [(c, pr) for c in range(n_chunks) for pr in range(n_pairs)]
    blk = lambda z, c, pr: z[c * CHUNK:(c + 1) * CHUNK, pr * LANES:(pr + 1) * LANES]
    afs = [blk(a_f, c, pr) for c, pr in inst]
    rfs = [blk(r_f, c, pr) for c, pr in inst]
    vfs = [blk(v, c, pr) for c, pr in inst]
    vts = [_bf(vf) for vf in vfs]
    bks = [jnp.concatenate([blk(b_t, c, pr), blk(k_t, c, pr)], axis=0) for c, pr in inst]
    kbs = [jnp.concatenate([blk(k_t, c, pr), blk(b_t, c, pr)], axis=0) for c, pr in inst]
    gh0s = [_dot_nt(jnp.concatenate([jnp.where(first, _bf(af), 0), jnp.where(first, _bf(rf), 0)],
                                    axis=0), bk) for af, rf, bk in zip(afs, rfs, bks)]
    gh1s = [_dot_nt(jnp.concatenate([jnp.where(first, 0, _bf(af)), jnp.where(first, 0, _bf(rf))],
                                    axis=0), kb) for af, rf, kb in zip(afs, rfs, kbs)]
    g0s = [jnp.where(strict, gh[:CHUNK], 0.0) for gh in gh0s]
    g1s = [jnp.where(strict, gh[:CHUNK], 0.0) for gh in gh1s]
    lhss = [jnp.concatenate(
        [jnp.concatenate([_bf(jnp.where(incl, ga[CHUNK:], 0.0)), z16], axis=1),
         jnp.concatenate([z16, _bf(jnp.where(incl, gb[CHUNK:], 0.0))], axis=1)], axis=0)
        for ga, gb in zip(gh0s, gh1s)]
    n_blks = [jnp.concatenate([jnp.where(first, g0, 0.0), jnp.where(first, 0.0, g1)], axis=0)
              for g0, g1 in zip(g0s, g1s)]
    akvs = [_dot(_bf(jnp.concatenate([jnp.where(first, 0.0, g0), jnp.where(first, g1, 0.0)],
                                     axis=0)), jnp.concatenate([vt, vt], axis=0))
            for g0, g1, vt in zip(g0s, g1s, vts)]
    xs = [jnp.concatenate([jnp.concatenate([af, af], axis=0), akv], axis=1)
          for af, akv in zip(afs, akvs)]
    xs = _solve_unit_lower(n_blks, xs)
    lows = [jnp.concatenate([jnp.where(first2, x[:CHUNK], x[CHUNK:]),
                             jnp.concatenate([z32, vf], axis=1)], axis=0)
            for x, vf in zip(xs, vfs)]
    lowbs = [_bf(low) for low in lows]
    outs = [_dot(lhs, jnp.concatenate([lowb, lowb[CHUNK:], lowb[:CHUNK]], axis=0))
            for lhs, lowb in zip(lhss, lowbs)]
    sels = [jnp.where(first2, out[:CHUNK], out[CHUNK:]) for out in outs]
    mcs = [_dot(_bf(low.T), bk) for low, bk in zip(lows, bks)]

    states = [s_scr[pr] for pr in range(n_pairs)]
    y_rows = []
    for c in range(n_chunks):
        y_pairs = []
        for pr in range(n_pairs):
            i = c * n_pairs + pr
            w_last = dec_in[(c + 1) * CHUNK - 1:(c + 1) * CHUNK, pr * LANES:(pr + 1) * LANES]
            g_tilde = rfs[i] + sels[i][:, :LANES]
            y_intra = sels[i][:, LANES:]
            m_lr = jnp.where(same_head, mcs[i][:LANES], 0.0) * w_last
            c_mat = jnp.where(same_head, mcs[i][LANES:], 0.0) * w_last
            s = states[pr]
            sb = _bf(s)
            y_pairs.append(_dot_nt(_bf(g_tilde), sb) + y_intra)
            states[pr] = s * w_last + _dot(sb, _bf(m_lr)) + c_mat
        y_rows.append(jnp.concatenate(y_pairs, axis=-1))
    for pr in range(n_pairs):
        s_scr[pr] = states[pr]
    y = jnp.concatenate(y_rows, axis=0)

    inv_n = 1.0 / HEAD
    mean = _dot_hilo(y, hsum) * inv_n
    dev = y - mean
    var = _dot_hilo(dev * dev, hsum) * inv_n
    yn = dev * lax.rsqrt(var + GN_EPS) * lng_ref[...] + lnb_ref[...]
    bonus = _dot_hilo(r * k_h * rk_ref[...], hsum) * v
    o_ref[0, r0:r0 + rows] = ((yn + bonus) * gate).astype(o_ref.dtype)


def _rwkv(pa, mu, w0, w2, a0, a2, g2, k_k, k_a, r_k, lnx_g, lnx_b, *, rows=RW_ROWS):
    b, t, ap = pa.shape
    rows = min(rows, t)
    assert rows % RW_ROWS == 0
    width = w0.shape[0]
    w_lora, a_lora = w2.shape[0], a2.shape[0]
    assert w_lora == HEAD and a_lora == HEAD and width % LANES == 0
    n_pairs = width // LANES
    wwa = jnp.zeros((w_lora + a_lora, 2 * width), F32)
    wwa = wwa.at[:w_lora, :width].set(w2).at[w_lora:, width:].set(a2)
    ti = jnp.arange(RW_ROWS)
    tri = ((ti[None, :] <= ti[:, None])
           & (ti[None, :] // CHUNK == ti[:, None] // CHUNK)).astype(BF16)
    hi = jnp.arange(width) // HEAD
    hsum = (hi[:, None] == hi[None, :]).astype(BF16)
    vec = lambda z: z.reshape(1, width)
    return pl.pallas_call(
        _rwkv_body,
        grid=(b, t // rows),
        in_specs=[
            pl.BlockSpec((1, rows, ap), lambda i, c: (i, c, 0)),
            _const_spec((1, ap)),
            _const_spec((1, width)), _const_spec((1, width)), _const_spec((1, width)),
            _const_spec((1, width)), _const_spec((1, width)), _const_spec((1, width)),
            _const_spec((1, width)),
            _const_spec((w_lora + a_lora, 2 * width)),
            _const_spec(g2.shape),
            _const_spec((RW_ROWS, RW_ROWS)),
            _const_spec((width, width)),
        ],
        out_specs=pl.BlockSpec((1, rows, width), lambda i, c: (i, c, 0)),
        out_shape=jax.ShapeDtypeStruct((b, t, width), BF16),
        scratch_shapes=[pltpu.VMEM((1, ap), F32), pltpu.VMEM((n_pairs, LANES, LANES), F32)],
        compiler_params=_params(2),
        name="rwkv7",
    )(pa, mu.reshape(1, ap), vec(w0), vec(a0), vec(k_k), vec(k_a), vec(r_k), vec(lnx_g),
      vec(lnx_b), _bf(wwa), _bf(g2), tri, hsum)


def _rglru_body(xb_ref, gb_ref, cw_ref, cb_ref, wg_ref, bg_ref, lam_ref, o_ref,
                prev_scr, h_scr):
    i = pl.program_id(1)
    tt, width = o_ref.shape[1], o_ref.shape[2]

    @pl.when(i == 0)
    def _():
        prev_scr[...] = jnp.zeros_like(prev_scr)
        h_scr[...] = jnp.zeros_like(h_scr)

    x = xb_ref[0]
    prev = prev_scr[...]
    prev_scr[...] = x[tt - 8:, :]
    row8 = lax.broadcasted_iota(jnp.int32, (8, width), 0)

    def delayed(d):
        rolled = pltpu.roll(x, d, axis=0)
        head = jnp.where(row8 < d, pltpu.roll(prev, d, axis=0), rolled[:8])
        return jnp.concatenate([head, rolled[8:]], axis=0)

    xc = cb_ref[...] + cw_ref[CONV_WIDTH - 1:CONV_WIDTH, :] * x
    for d in range(1, CONV_WIDTH):
        xc = xc + cw_ref[CONV_WIDTH - 1 - d:CONV_WIDTH - d, :] * delayed(d)

    gates = _dot(_bf(xc), wg_ref[...]) + bg_ref[...]
    r_g = _sigmoid(gates[:, :width])
    i_g = _sigmoid(gates[:, width:])
    log_a = -LRU_C * r_g * _softplus(-lam_ref[...])
    a = jnp.exp(log_a)
    mult = jnp.sqrt(jnp.tanh(-log_a) * (1.0 + a * a))
    u = mult * i_g * xc

    a = a.reshape(tt // SUBLANES, SUBLANES, width)
    u = u.reshape(tt // SUBLANES, SUBLANES, width)
    sub = lax.broadcasted_iota(jnp.int32, a.shape, 1)
    d = 1
    while d < SUBLANES:
        keep = sub >= d
        a_sh = jnp.where(keep, pltpu.roll(a, d, axis=1), 1.0)
        u_sh = jnp.where(keep, pltpu.roll(u, d, axis=1), 0.0)
        u = a * u_sh + u
        a = a * a_sh
        d *= 2
    carry = h_scr[...]
    groups = []
    for g in range(tt // SUBLANES):
        hg = u[g] + a[g] * carry
        groups.append(hg)
        carry = hg[SUBLANES - 1:]
    h = jnp.concatenate(groups, axis=0)
    h_scr[...] = carry
    o_ref[0] = (h * _gelu_tanh(gb_ref[0])).astype(o_ref.dtype)


def _rglru(xb, gb, conv_w, conv_b, gate_a_w, gate_a_b, gate_x_w, gate_x_b, lam, *, tt=256):
    b, t, width = xb.shape
    tt = min(tt, t)
    nb, bd, _ = gate_a_w.shape
    eye = jnp.eye(nb, dtype=F32)
    dense = lambda w: jnp.einsum("nij,nm->nimj", w, eye).reshape(width, width)
    wg = _bf(jnp.concatenate([dense(gate_a_w), dense(gate_x_w)], axis=1))
    bg = jnp.concatenate([gate_a_b, gate_x_b]).reshape(1, 2 * width)
    tile = pl.BlockSpec((1, tt, width), lambda i, j: (i, j, 0))
    return pl.pallas_call(
        _rglru_body,
        grid=(b, t // tt),
        in_specs=[tile, tile, _const_spec((CONV_WIDTH, width)), _const_spec((1, width)),
                  _const_spec((width, 2 * width)), _const_spec((1, 2 * width)),
                  _const_spec((1, width))],
        out_specs=tile,
        out_shape=jax.ShapeDtypeStruct((b, t, width), BF16),
        scratch_shapes=[pltpu.VMEM((8, width), F32), pltpu.VMEM((1, width), F32)],
        compiler_params=_params(2),
        name="rglru",
    )(xb, gb, conv_w, conv_b.reshape(1, width), wg, bg, lam.reshape(1, width))


SB_DEPTH = 5
SB_ROWS = 64


def _sb_body(qkv_ref, later_ref, o_ref, zn_scr, cs_scr, lsig_scr, negb_scr, att_scr, acc_scr,
             run_scr, *, blk):
    n_p = qkv_ref.shape[1]
    n_q = qkv_ref.shape[2] // blk
    n_steps = n_p * n_q * (n_q + 1) // 2 + SB_DEPTH - 1
    first = lax.broadcasted_iota(jnp.int32, (blk, LANES), 1) < HEAD
    later = later_ref[...]
    rr = lax.broadcasted_iota(jnp.int32, (2 * blk, blk), 0)
    cc = lax.broadcasted_iota(jnp.int32, (2 * blk, blk), 1)
    ahead = cc - jnp.where(rr < blk, rr, rr - blk)

    def rows(which, p, i):
        return qkv_ref[which, p, pl.ds(pl.multiple_of(i * blk, blk), blk), :]

    def step(cur, triples):
        old = 1 - cur
        ps, qis, js = triples
        p0 = jnp.minimum(ps[0], n_p - 1)
        qa = rows(0, p0, jnp.minimum(qis[0], n_q - 1))
        q2 = jnp.concatenate([jnp.where(first, qa, 0), jnp.where(first, 0, qa)], axis=0)
        zn_scr[cur] = _dot_nt(q2, rows(1, p0, jnp.minimum(js[0], n_q - 1)))
        cs_scr[cur] = _dot(negb_scr[old], later)
        pv = _dot(att_scr[old], rows(2, ps[4], js[4]))
        for r in range(0, 2 * blk, SB_ROWS):
            rs = slice(r, r + SB_ROWS)
            att_scr[cur, rs] = _bf(jnp.exp2(lsig_scr[cur, rs] + cs_scr[old, rs]))
        for r in range(0, 2 * blk, SB_ROWS):
            rs = slice(r, r + SB_ROWS)
            znm = jnp.where(ahead[rs] < (qis[1] - js[1]) * blk, zn_scr[old, rs], 1e30)
            low = jnp.minimum(znm, 0.0)
            top = low - znm
            lse = jnp.log(1.0 + jnp.exp2(low + top)) * LOG2E
            neg = low - lse
            run = jnp.where(js[1] == qis[1], 0.0, run_scr[rs])
            negb_scr[cur, rs] = _bf(neg)
            lsig_scr[cur, rs] = (top - lse) + run
            run_scr[rs] = run + jnp.sum(neg, axis=-1, keepdims=True)
        acc = jnp.where(js[4] == qis[4], 0.0, acc_scr[...]) + pv
        acc_scr[...] = acc
        o_ref[ps[4], pl.ds(pl.multiple_of(qis[4] * blk, blk), blk), :] = jnp.where(
            first, acc[:blk], acc[blk:]).astype(o_ref.dtype)
        blk_done = js[0] == 0
        pair_done = jnp.logical_and(blk_done, qis[0] == n_q - 1)
        q_new = jnp.where(pair_done, 0, qis[0] + 1)
        p_next = jnp.where(pair_done, ps[0] + 1, ps[0])
        q_next = jnp.where(blk_done, q_new, qis[0])
        j_next = jnp.where(blk_done, q_new, js[0] - 1)
        return (p_next,) + ps[:-1], (q_next,) + qis[:-1], (j_next,) + js[:-1]

    for scr in (zn_scr, cs_scr, lsig_scr, negb_scr, att_scr, acc_scr, run_scr):
        scr[...] = jnp.zeros_like(scr)
    triples = ((jnp.int32(0),) * SB_DEPTH,) * 3
    if n_steps % 2:
        triples = step(1, triples)
    lax.fori_loop(0, n_steps // 2, lambda _, tr: step(1, step(0, tr)), triples)


def _stick_breaking(qkv, batch, *, blk=256):
    _, n_p, m, _ = qkv.shape
    t = m // batch
    blk = min(blk, t)
    ki = jnp.arange(blk)
    later = (ki[:, None] > ki[None, :]).astype(BF16)
    wide = lambda dt: pltpu.VMEM((2, 2 * blk, blk), dt)
    return pl.pallas_call(
        functools.partial(_sb_body, blk=blk),
        grid=(batch,),
        in_specs=[pl.BlockSpec((3, n_p, t, LANES), lambda i: (0, 0, i, 0)),
                  _const_spec((blk, blk))],
        out_specs=pl.BlockSpec((n_p, t, LANES), lambda i: (0, i, 0)),
        out_shape=jax.ShapeDtypeStruct((n_p, m, LANES), BF16),
        scratch_shapes=[wide(F32), wide(F32), wide(F32), wide(BF16), wide(BF16),
                        pltpu.VMEM((2 * blk, LANES), F32), pltpu.VMEM((2 * blk, 1), F32)],
        compiler_params=_params(1),
        name="stick_breaking",
    )(qkv, later)


def kernel(x, l0_ffn1_pre_g, l0_ffn1_post_g, l0_ffn1_w_in, l0_ffn1_w_out, l0_mix_pre_g, l0_mix_post_g, l0_w_in, l0_mu, l0_w0, l0_w2, l0_a0, l0_a2, l0_g2, l0_k_k, l0_k_a, l0_r_k, l0_lnx_g, l0_lnx_b, l0_conv_w, l0_conv_b, l0_gate_a_w, l0_gate_a_b, l0_gate_x_w, l0_gate_x_b, l0_lambda, l0_w_out, l0_ffn2_pre_g, l0_ffn2_post_g, l0_ffn2_w_in, l0_ffn2_w_out, l1_ffn1_pre_g, l1_ffn1_post_g, l1_ffn1_w_in, l1_ffn1_w_out, l1_mix_pre_g, l1_mix_post_g, l1_w_qkv, l1_w_out, l1_ffn2_pre_g, l1_ffn2_post_g, l1_ffn2_w_in, l1_ffn2_w_out):
    b, t, d = x.shape
    a_width = l0_w0.shape[0]
    b_width = l0_lambda.shape[0]
    a_proj = l0_mu.shape[0]
    h = x.reshape(b * t, d)

    h = _ffn(h, l0_ffn1_pre_g, l0_ffn1_post_g, l0_ffn1_w_in, l0_ffn1_w_out)
    pa, xb, gb = _proj(h, l0_mix_pre_g, l0_w_in, (a_proj, b_width, b_width), F32)
    ya = _rwkv(pa.reshape(b, t, a_proj), l0_mu, l0_w0, l0_w2, l0_a0, l0_a2, l0_g2, l0_k_k,
               l0_k_a, l0_r_k.reshape(-1), l0_lnx_g, l0_lnx_b)
    yb = _rglru(xb.reshape(b, t, b_width), gb.reshape(b, t, b_width), l0_conv_w, l0_conv_b,
                l0_gate_a_w, l0_gate_a_b, l0_gate_x_w, l0_gate_x_b, l0_lambda)
    h = _ffn(h, l0_ffn2_pre_g, l0_ffn2_post_g, l0_ffn2_w_in, l0_ffn2_w_out,
             mix=([ya.reshape(b * t, a_width), yb.reshape(b * t, b_width)],
                  [l0_w_out[:a_width], l0_w_out[a_width:]], l0_mix_post_g))

    h = _ffn(h, l1_ffn1_pre_g, l1_ffn1_post_g, l1_ffn1_w_in, l1_ffn1_w_out)
    qkv = _proj_tiles(h, l1_mix_pre_g, l1_w_qkv, 3, BF16,
                      scales=(-LOG2E / math.sqrt(HEAD), None, None))
    o = _stick_breaking(qkv, b)
    h = _ffn(h, l1_ffn2_pre_g, l1_ffn2_post_g, l1_ffn2_w_in, l1_ffn2_w_out,
             mix=([o], [l1_w_out], l1_mix_post_g))
    return h.reshape(b, t, d)
```

```python
import functools
import math

import jax
import jax.numpy as jnp
from jax import lax
from jax.experimental import pallas as pl
from jax.experimental.pallas import tpu as pltpu

F32 = jnp.float32
BF16 = jnp.bfloat16

NORM_EPS = 1e-6
GN_EPS = 64e-5
LRU_C = 8.0
HEAD = 64
LANES = 128
SUBLANES = 8
CHUNK = 64
RW_ROWS = 256
LRU_SLICES = 1
CONV_WIDTH = 4
LOG2E = 1.0 / math.log(2.0)
VMEM_LIMIT = 56 * 1024 * 1024


def _params(n_axes, vmem=VMEM_LIMIT):
    return pltpu.CompilerParams(
        dimension_semantics=("arbitrary",) * n_axes, vmem_limit_bytes=vmem)


def _bf(x):
    return x.astype(BF16)


def _dot(a, b):
    return jnp.dot(a, b, preferred_element_type=F32)


def _dot_nt(a, b):
    return lax.dot_general(a, b, (((1,), (1,)), ((), ())), preferred_element_type=F32)


def _dot_hilo(x, e):
    hi = _bf(x)
    lo = _bf(x - hi.astype(F32))
    return _dot(hi, e) + _dot(lo, e)


def _rms(x, g):
    return x * lax.rsqrt(jnp.mean(x * x, axis=-1, keepdims=True) + NORM_EPS) * g


def _sigmoid(x):
    return 1.0 / (1.0 + jnp.exp(-x))


def _softplus(x):
    return jnp.maximum(x, 0.0) + jnp.log1p(jnp.exp(-jnp.abs(x)))


def _gelu_tanh(x):
    c = math.sqrt(2.0 / math.pi)
    return 0.5 * x * (1.0 + jnp.tanh(c * (x + 0.044715 * (x * x * x))))


def _row_spec(tm, n):
    return pl.BlockSpec((tm, n), lambda i, *_: (i, 0))


def _const_spec(shape):
    return pl.BlockSpec(shape, lambda *_: (0,) * len(shape))


def _act_rows(a_ref):
    if len(a_ref.shape) == 2:
        return a_ref[...]
    return jnp.concatenate([a_ref[p] for p in range(a_ref.shape[0])], axis=1)


def _ffn_body(*refs, tf, n_mix):
    a_refs, w_refs = refs[:n_mix], refs[n_mix:2 * n_mix]
    rest = refs[2 * n_mix:]
    if n_mix:
        gmix_ref, rest = rest[0], rest[1:]
    x_ref, gpre_ref, gpost_ref, win_ref, wo_ref, o_ref = rest
    f = wo_ref.shape[0]
    x = x_ref[...]
    if n_mix:
        y = _dot(_act_rows(a_refs[0]), w_refs[0][...])
        for a_ref, w_ref in zip(a_refs[1:], w_refs[1:]):
            y += _dot(_act_rows(a_ref), w_ref[...])
        x = x + _rms(y, gmix_ref[...])
    h = _bf(_rms(x, gpre_ref[...]))

    def gate_up(c):
        return (_dot(h, win_ref[:, c * tf:(c + 1) * tf]),
                _dot(h, win_ref[:, f + c * tf:f + (c + 1) * tf]))

    n = f // tf
    acc = None
    gate, up = gate_up(0)
    for c in range(n):
        act = _bf(gate * _sigmoid(gate) * up)
        if c + 1 < n:
            gate, up = gate_up(c + 1)
        part = _dot(act, wo_ref[c * tf:(c + 1) * tf, :])
        acc = part if acc is None else acc + part
    o_ref[...] = x + 0.5 * _rms(acc, gpost_ref[...])


def _ffn(x, g_pre, g_post, w_in, w_out, *, mix=None, tm=512, tf=256):
    m, d = x.shape
    tm = min(tm, m)
    f = w_out.shape[0]
    acts, ws, g_mix = mix if mix is not None else ((), (), None)
    resident = lambda shape: pl.BlockSpec(shape, lambda i: (0, 0), pipeline_mode=pl.Buffered(1))
    act_spec = lambda a: (_row_spec(tm, a.shape[1]) if a.ndim == 2 else
                          pl.BlockSpec((a.shape[0], tm, a.shape[2]), lambda i: (0, i, 0)))
    mix_specs = [act_spec(a) for a in acts] + [resident(w.shape) for w in ws]
    mix_args = list(acts) + [_bf(w) for w in ws]
    if mix is not None:
        mix_specs.append(_const_spec((1, d)))
        mix_args.append(g_mix.reshape(1, d))
    return pl.pallas_call(
        functools.partial(_ffn_body, tf=tf, n_mix=len(acts)),
        grid=(m // tm,),
        in_specs=mix_specs + [_row_spec(tm, d), _const_spec((1, d)), _const_spec((1, d)),
                              resident((d, 2 * f)), resident((f, d))],
        out_specs=_row_spec(tm, d),
        out_shape=jax.ShapeDtypeStruct((m, d), F32),
        compiler_params=_params(1),
        name="ffn",
    )(*mix_args, x, g_pre.reshape(1, d), g_post.reshape(1, d), _bf(w_in), _bf(w_out))


def _proj_body(x_ref, g_ref, w_ref, *o_refs, scales):
    h = _bf(_rms(x_ref[...], g_ref[...]))
    y = _dot(h, w_ref[...])
    off = 0
    for o_ref, scale in zip(o_refs, scales):
        n = o_ref.shape[-1]
        part = y[:, off:off + n]
        if scale is not None:
            part = part * scale
        o_ref[...] = part.astype(o_ref.dtype)
        off += n


def _proj(x, g, w, splits, dtype, *, scales=None, tm=512):
    m, d = x.shape
    tm = min(tm, m)
    n = w.shape[1]
    assert sum(splits) == n
    scales = scales or (None,) * len(splits)
    return pl.pallas_call(
        functools.partial(_proj_body, scales=scales),
        grid=(m // tm,),
        in_specs=[_row_spec(tm, d), _const_spec((1, d)), _const_spec((d, n))],
        out_specs=[_row_spec(tm, s) for s in splits],
        out_shape=[jax.ShapeDtypeStruct((m, s), dtype) for s in splits],
        compiler_params=_params(1),
        name="proj",
    )(x, g.reshape(1, d), _bf(w))


def _proj_tiles_body(x_ref, g_ref, w_ref, o_ref, *, scales):
    h = _bf(_rms(x_ref[...], g_ref[...]))
    y = _dot(h, w_ref[...])
    n_groups, n_tiles = o_ref.shape[0], o_ref.shape[1]
    for s in range(n_groups):
        for p in range(n_tiles):
            off = (s * n_tiles + p) * LANES
            part = y[:, off:off + LANES]
            if scales[s] is not None:
                part = part * scales[s]
            o_ref[s, p] = part.astype(o_ref.dtype)


def _proj_tiles(x, g, w, n_groups, dtype, *, scales, tm=512):
    m, d = x.shape
    tm = min(tm, m)
    n = w.shape[1]
    n_tiles = n // (n_groups * LANES)
    return pl.pallas_call(
        functools.partial(_proj_tiles_body, scales=scales),
        grid=(m // tm,),
        in_specs=[_row_spec(tm, d), _const_spec((1, d)), _const_spec((d, n))],
        out_specs=pl.BlockSpec((n_groups, n_tiles, tm, LANES), lambda i: (0, 0, i, 0)),
        out_shape=jax.ShapeDtypeStruct((n_groups, n_tiles, m, LANES), dtype),
        compiler_params=_params(1),
        name="proj_tiles",
    )(x, g.reshape(1, d), _bf(w))


def _solve_unit_lower(ns, xs, fills=()):
    ps = [_bf(n) for n in ns]
    steps = int(math.log2(CHUNK))
    for i in range(steps):
        if i < len(fills):
            fills[i]()
        xs = [x + _dot(p, _bf(x)) for p, x in zip(ps, xs)]
        if i + 1 < steps:
            ps = [_bf(_dot(p, p)) for p in ps]
    return xs


def _rwkv_rows(r0, pa_ref, mu_ref, w0_ref, a0_ref, kk_ref, ka_ref, rk_ref, lng_ref, lnb_ref,
               wwa_ref, g2_ref, tri_ref, hsum_ref, o_ref, prev_scr, s_scr, *, fills):
    rows, width = RW_ROWS, o_ref.shape[2]
    n_pairs = width // LANES
    n_chunks = rows // CHUNK
    p = pa_ref[0, r0:r0 + rows]
    row = lax.broadcasted_iota(jnp.int32, p.shape, 0)
    shifted = jnp.where(row == 0, prev_scr[...], pltpu.roll(p, 1, axis=0))
    prev_scr[...] = p[rows - 1:rows, :]
    xm = p + mu_ref[...] * (shifted - p)

    r = xm[:, 0:width]
    k = xm[:, width:2 * width]
    v = xm[:, 2 * width:3 * width]
    wa_lo = xm[:, 3 * width:3 * width + LANES]
    g_lo = xm[:, 3 * width + LANES:]
    lane_wa = lax.broadcasted_iota(jnp.int32, wa_lo.shape, 1)
    wa_act = jnp.where(lane_wa < LANES // 2, jnp.tanh(wa_lo), wa_lo)
    lora = _dot(_bf(wa_act), wwa_ref[...])
    w_log = -_softplus(-(w0_ref[...] + lora[:, :width])) - 0.5
    log_decay = -jnp.exp(w_log)
    a = _sigmoid(a0_ref[...] + lora[:, width:])
    gate = _dot(_bf(_sigmoid(g_lo)), g2_ref[...])

    hsum = hsum_ref[...]
    kk = k * kk_ref[...]
    kk = kk / jnp.maximum(jnp.sqrt(_dot_hilo(kk * kk, hsum)), 1e-12)
    k_h = k * (1.0 + (a - 1.0) * ka_ref[...])

    ld_hi = _bf(log_decay)
    ld_lo = _bf(log_decay - ld_hi.astype(F32))
    tri = tri_ref[...]
    cum = _dot(tri, ld_hi) + _dot(tri, ld_lo)
    dec_in = jnp.exp(cum)
    dec_out = jnp.exp(-cum)
    r_f = r * dec_in
    a_f = -kk * jnp.exp(cum - log_decay)
    b_t = _bf(kk * a * dec_out)
    k_t = _bf(k_h * dec_out)

    lane = lax.broadcasted_iota(jnp.int32, (CHUNK, LANES), 1)
    trow = lax.broadcasted_iota(jnp.int32, (CHUNK, LANES), 0)
    first = lane < HEAD
    src = jnp.where(first, lane, lane - HEAD)
    strict = src < trow
    incl = src <= trow
    lane2 = lax.broadcasted_iota(jnp.int32, (CHUNK, 2 * LANES), 1)
    first2 = jnp.where(lane2 < LANES, lane2, lane2 - LANES) < HEAD
    sr = lax.broadcasted_iota(jnp.int32, (LANES, LANES), 0)
    sc = lax.broadcasted_iota(jnp.int32, (LANES, LANES), 1)
    same_head = (sr < HEAD) == (sc < HEAD)
    z16 = jnp.zeros((CHUNK, LANES), BF16)
    z32 = jnp.zeros((CHUNK, LANES), F32)

    inst = [(c, pr) for c in range(n_chunks) for pr in range(n_pairs)]
    blk = lambda z, c, pr: z[c * CHUNK:(c + 1) * CHUNK, pr * LANES:(pr + 1) * LANES]
    afs = [blk(a_f, c, pr) for c, pr in inst]
    rfs = [blk(r_f, c, pr) for c, pr in inst]
    vfs = [blk(v, c, pr) for c, pr in inst]
    vts = [_bf(vf) for vf in vfs]
    bks = [jnp.concatenate([blk(b_t, c, pr), blk(k_t, c, pr)], axis=0) for c, pr in inst]
    kbs = [jnp.concatenate([blk(k_t, c, pr), blk(b_t, c, pr)], axis=0) for c, pr in inst]
    gh0s = [_dot_nt(jnp.concatenate([jnp.where(first, _bf(af), 0), jnp.where(first, _bf(rf), 0)],
                                    axis=0), bk) for af, rf, bk in zip(afs, rfs, bks)]
    gh1s = [_dot_nt(jnp.concatenate([jnp.where(first, 0, _bf(af)), jnp.where(first, 0, _bf(rf))],
                                    axis=0), kb) for af, rf, kb in zip(afs, rfs, kbs)]
    g0s = [jnp.where(strict, gh[:CHUNK], 0.0) for gh in gh0s]
    g1s = [jnp.where(strict, gh[:CHUNK], 0.0) for gh in gh1s]
    lhss = [jnp.concatenate(
        [jnp.concatenate([_bf(jnp.where(incl, ga[CHUNK:], 0.0)), z16], axis=1),
         jnp.concatenate([z16, _bf(jnp.where(incl, gb[CHUNK:], 0.0))], axis=1)], axis=0)
        for ga, gb in zip(gh0s, gh1s)]
    n_blks = [jnp.concatenate([jnp.where(first, g0, 0.0), jnp.where(first, 0.0, g1)], axis=0)
              for g0, g1 in zip(g0s, g1s)]
    akvs = [_dot(_bf(jnp.concatenate([jnp.where(first, 0.0, g0), jnp.where(first, g1, 0.0)],
                                     axis=0)), jnp.concatenate([vt, vt], axis=0))
            for g0, g1, vt in zip(g0s, g1s, vts)]
    xs = [jnp.concatenate([jnp.concatenate([af, af], axis=0), akv], axis=1)
          for af, akv in zip(afs, akvs)]
    xs = _solve_unit_lower(n_blks, xs, fills)
    lows = [jnp.concatenate([jnp.where(first2, x[:CHUNK], x[CHUNK:]),
                             jnp.concatenate([z32, vf], axis=1)], axis=0)
            for x, vf in zip(xs, vfs)]
    lowbs = [_bf(low) for low in lows]
    outs = [_dot(lhs, jnp.concatenate([lowb, lowb[CHUNK:], lowb[:CHUNK]], axis=0))
            for lhs, lowb in zip(lhss, lowbs)]
    sels = [jnp.where(first2, out[:CHUNK], out[CHUNK:]) for out in outs]
    mcs = [_dot(_bf(low.T), bk) for low, bk in zip(lows, bks)]

    states = [s_scr[pr] for pr in range(n_pairs)]
    y_rows = []
    for c in range(n_chunks):
        y_pairs = []
        for pr in range(n_pairs):
            i = c * n_pairs + pr
            w_last = dec_in[(c + 1) * CHUNK - 1:(c + 1) * CHUNK, pr * LANES:(pr + 1) * LANES]
            g_tilde = rfs[i] + sels[i][:, :LANES]
            y_intra = sels[i][:, LANES:]
            m_lr = jnp.where(same_head, mcs[i][:LANES], 0.0) * w_last
            c_mat = jnp.where(same_head, mcs[i][LANES:], 0.0) * w_last
            s = states[pr]
            sb = _bf(s)
            y_pairs.append(_dot_nt(_bf(g_tilde), sb) + y_intra)
            states[pr] = s * w_last + _dot(sb, _bf(m_lr)) + c_mat
        y_rows.append(jnp.concatenate(y_pairs, axis=-1))
    for pr in range(n_pairs):
        s_scr[pr] = states[pr]
    y = jnp.concatenate(y_rows, axis=0)

    inv_n = 1.0 / HEAD
    mean = _dot_hilo(y, hsum) * inv_n
    dev = y - mean
    var = _dot_hilo(dev * dev, hsum) * inv_n
    yn = dev * lax.rsqrt(var + GN_EPS) * lng_ref[...] + lnb_ref[...]
    bonus = _dot_hilo(r * k_h * rk_ref[...], hsum) * v
    o_ref[0, r0:r0 + rows] = ((yn + bonus) * gate).astype(o_ref.dtype)


def _rwkv_call(pa, mu, w0, w2, a0, a2, g2, k_k, k_a, r_k, lnx_g, lnx_b, *, rows):
    b, t, ap = pa.shape
    assert rows % RW_ROWS == 0
    width = w0.shape[0]
    w_lora, a_lora = w2.shape[0], a2.shape[0]
    assert w_lora == HEAD and a_lora == HEAD and width % LANES == 0
    n_pairs = width // LANES
    wwa = jnp.zeros((w_lora + a_lora, 2 * width), F32)
    wwa = wwa.at[:w_lora, :width].set(w2).at[w_lora:, width:].set(a2)
    ti = jnp.arange(RW_ROWS)
    tri = ((ti[None, :] <= ti[:, None])
           & (ti[None, :] // CHUNK == ti[:, None] // CHUNK)).astype(BF16)
    hi = jnp.arange(width) // HEAD
    hsum = (hi[:, None] == hi[None, :]).astype(BF16)
    vec = lambda z: z.reshape(1, width)
    args = [pa, mu.reshape(1, ap), vec(w0), vec(a0), vec(k_k), vec(k_a), vec(r_k), vec(lnx_g),
            vec(lnx_b), _bf(wwa), _bf(g2), tri, hsum]
    in_specs = ([pl.BlockSpec((1, rows, ap), lambda i, c: (i, c, 0)), _const_spec((1, ap))]
                + [_const_spec((1, width))] * 7
                + [_const_spec((w_lora + a_lora, 2 * width)), _const_spec(g2.shape),
                   _const_spec((RW_ROWS, RW_ROWS)), _const_spec((width, width))])
    out_spec = pl.BlockSpec((1, rows, width), lambda i, c: (i, c, 0))
    out_shape = jax.ShapeDtypeStruct((b, t, width), BF16)
    scratch = [pltpu.VMEM((1, ap), F32), pltpu.VMEM((n_pairs, LANES, LANES), F32)]
    return args, in_specs, out_spec, out_shape, scratch


def _rglru_rows(r0, tt, xb_ref, gb_ref, cw_ref, cb_ref, wg_ref, bg_ref, lam_ref, o_ref,
                prev_scr, h_scr):
    width = o_ref.shape[2]
    x = xb_ref[0, r0:r0 + tt]
    prev = prev_scr[...]
    prev_scr[...] = x[tt - 8:, :]
    row8 = lax.broadcasted_iota(jnp.int32, (8, width), 0)

    def delayed(d):
        rolled = pltpu.roll(x, d, axis=0)
        head = jnp.where(row8 < d, pltpu.roll(prev, d, axis=0), rolled[:8])
        return jnp.concatenate([head, rolled[8:]], axis=0)

    xc = cb_ref[...] + cw_ref[CONV_WIDTH - 1:CONV_WIDTH, :] * x
    for d in range(1, CONV_WIDTH):
        xc = xc + cw_ref[CONV_WIDTH - 1 - d:CONV_WIDTH - d, :] * delayed(d)

    gates = _dot(_bf(xc), wg_ref[...]) + bg_ref[...]
    r_g = _sigmoid(gates[:, :width])
    i_g = _sigmoid(gates[:, width:])
    log_a = -LRU_C * r_g * _softplus(-lam_ref[...])
    a = jnp.exp(log_a)
    mult = jnp.sqrt(jnp.tanh(-log_a) * (1.0 + a * a))
    u = mult * i_g * xc

    a = a.reshape(tt // SUBLANES, SUBLANES, width)
    u = u.reshape(tt // SUBLANES, SUBLANES, width)
    sub = lax.broadcasted_iota(jnp.int32, a.shape, 1)
    d = 1
    while d < SUBLANES:
        keep = sub >= d
        a_sh = jnp.where(keep, pltpu.roll(a, d, axis=1), 1.0)
        u_sh = jnp.where(keep, pltpu.roll(u, d, axis=1), 0.0)
        u = a * u_sh + u
        a = a * a_sh
        d *= 2
    carry = h_scr[...]
    groups = []
    for g in range(tt // SUBLANES):
        hg = u[g] + a[g] * carry
        groups.append(hg)
        carry = hg[SUBLANES - 1:]
    h = jnp.concatenate(groups, axis=0)
    h_scr[...] = carry
    o_ref[0, r0:r0 + tt] = (h * _gelu_tanh(gb_ref[0, r0:r0 + tt])).astype(o_ref.dtype)


def _rglru_call(xb, gb, conv_w, conv_b, gate_a_w, gate_a_b, gate_x_w, gate_x_b, lam, *, tt):
    b, t, width = xb.shape
    nb, bd, _ = gate_a_w.shape
    eye = jnp.eye(nb, dtype=F32)
    dense = lambda w: jnp.einsum("nij,nm->nimj", w, eye).reshape(width, width)
    wg = _bf(jnp.concatenate([dense(gate_a_w), dense(gate_x_w)], axis=1))
    bg = jnp.concatenate([gate_a_b, gate_x_b]).reshape(1, 2 * width)
    tile = pl.BlockSpec((1, tt, width), lambda i, j: (i, j, 0))
    args = [xb, gb, conv_w, conv_b.reshape(1, width), wg, bg, lam.reshape(1, width)]
    in_specs = [tile, tile, _const_spec((CONV_WIDTH, width)), _const_spec((1, width)),
                _const_spec((width, 2 * width)), _const_spec((1, 2 * width)),
                _const_spec((1, width))]
    out_shape = jax.ShapeDtypeStruct((b, t, width), BF16)
    scratch = [pltpu.VMEM((SUBLANES, width), F32), pltpu.VMEM((1, width), F32)]
    return args, in_specs, tile, out_shape, scratch


def _mixers0_body(*refs, n_a, n_b):
    a_in, b_in = refs[:n_a], refs[n_a:n_a + n_b]
    ya_ref, yb_ref, a_prev, a_state, b_prev, b_state = refs[n_a + n_b:]

    @pl.when(pl.program_id(1) == 0)
    def _():
        for scr in (a_prev, a_state, b_prev, b_state):
            scr[...] = jnp.zeros_like(scr)

    tt = yb_ref.shape[1] // LRU_SLICES
    fills = [functools.partial(_rglru_rows, s * tt, tt, *b_in, yb_ref, b_prev, b_state)
             for s in range(LRU_SLICES)]
    _rwkv_rows(0, *a_in, ya_ref, a_prev, a_state, fills=fills)


def _mixers0(rwkv_args, rglru_args, *, rows=RW_ROWS):
    b, t, _ = rwkv_args[0].shape
    rows = min(rows, t)
    a = _rwkv_call(*rwkv_args, rows=rows)
    lru = _rglru_call(*rglru_args, tt=rows)
    return pl.pallas_call(
        functools.partial(_mixers0_body, n_a=len(a[0]), n_b=len(lru[0])),
        grid=(b, t // rows),
        in_specs=a[1] + lru[1],
        out_specs=[a[2], lru[2]],
        out_shape=[a[3], lru[3]],
        scratch_shapes=a[4] + lru[4],
        compiler_params=_params(2),
        name="mixers0",
    )(*a[0], *lru[0])


SB_DEPTH = 5
SB_ROWS = 64


def _sb_body(qkv_ref, later_ref, o_ref, zn_scr, cs_scr, lsig_scr, negb_scr, att_scr, acc_scr,
             run_scr, *, blk):
    n_p = qkv_ref.shape[1]
    n_q = qkv_ref.shape[2] // blk
    n_steps = n_p * n_q * (n_q + 1) // 2 + SB_DEPTH - 1
    first = lax.broadcasted_iota(jnp.int32, (blk, LANES), 1) < HEAD
    later = later_ref[...]
    rr = lax.broadcasted_iota(jnp.int32, (2 * blk, blk), 0)
    cc = lax.broadcasted_iota(jnp.int32, (2 * blk, blk), 1)
    ahead = cc - jnp.where(rr < blk, rr, rr - blk)

    def rows(which, p, i):
        return qkv_ref[which, p, pl.ds(pl.multiple_of(i * blk, blk), blk), :]

    def step(cur, triples):
        old = 1 - cur
        ps, qis, js = triples
        p0 = jnp.minimum(ps[0], n_p - 1)
        qa = rows(0, p0, jnp.minimum(qis[0], n_q - 1))
        q2 = jnp.concatenate([jnp.where(first, qa, 0), jnp.where(first, 0, qa)], axis=0)
        zn_scr[cur] = _dot_nt(q2, rows(1, p0, jnp.minimum(js[0], n_q - 1)))
        cs_scr[cur] = _dot(negb_scr[old], later)
        pv = _dot(att_scr[old], rows(2, ps[4], js[4]))
        for r in range(0, 2 * blk, SB_ROWS):
            rs = slice(r, r + SB_ROWS)
            att_scr[cur, rs] = _bf(jnp.exp2(lsig_scr[cur, rs] + cs_scr[old, rs]))
        for r in range(0, 2 * blk, SB_ROWS):
            rs = slice(r, r + SB_ROWS)
            znm = jnp.where(ahead[rs] < (qis[1] - js[1]) * blk, zn_scr[old, rs], 1e30)
            low = jnp.minimum(znm, 0.0)
            top = low - znm
            lse = jnp.log(1.0 + jnp.exp2(low + top)) * LOG2E
            neg = low - lse
            run = jnp.where(js[1] == qis[1], 0.0, run_scr[rs])
            negb_scr[cur, rs] = _bf(neg)
            lsig_scr[cur, rs] = (top - lse) + run
            run_scr[rs] = run + jnp.sum(neg, axis=-1, keepdims=True)
        acc = jnp.where(js[4] == qis[4], 0.0, acc_scr[...]) + pv
        acc_scr[...] = acc
        o_ref[ps[4], pl.ds(pl.multiple_of(qis[4] * blk, blk), blk), :] = jnp.where(
            first, acc[:blk], acc[blk:]).astype(o_ref.dtype)
        blk_done = js[0] == 0
        pair_done = jnp.logical_and(blk_done, qis[0] == n_q - 1)
        q_new = jnp.where(pair_done, 0, qis[0] + 1)
        p_next = jnp.where(pair_done, ps[0] + 1, ps[0])
        q_next = jnp.where(blk_done, q_new, qis[0])
        j_next = jnp.where(blk_done, q_new, js[0] - 1)
        return (p_next,) + ps[:-1], (q_next,) + qis[:-1], (j_next,) + js[:-1]

    for scr in (zn_scr, cs_scr, lsig_scr, negb_scr, att_scr, acc_scr, run_scr):
        scr[...] = jnp.zeros_like(scr)
    triples = ((jnp.int32(0),) * SB_DEPTH,) * 3
    if n_steps % 2:
        triples = step(1, triples)
    lax.fori_loop(0, n_steps // 2, lambda _, tr: step(1, step(0, tr)), triples)


def _stick_breaking(qkv, batch, *, blk=256):
    _, n_p, m, _ = qkv.shape
    t = m // batch
    blk = min(blk, t)
    ki = jnp.arange(blk)
    later = (ki[:, None] > ki[None, :]).astype(BF16)
    wide = lambda dt: pltpu.VMEM((2, 2 * blk, blk), dt)
    return pl.pallas_call(
        functools.partial(_sb_body, blk=blk),
        grid=(batch,),
        in_specs=[pl.BlockSpec((3, n_p, t, LANES), lambda i: (0, 0, i, 0)),
                  _const_spec((blk, blk))],
        out_specs=pl.BlockSpec((n_p, t, LANES), lambda i: (0, i, 0)),
        out_shape=jax.ShapeDtypeStruct((n_p, m, LANES), BF16),
        scratch_shapes=[wide(F32), wide(F32), wide(F32), wide(BF16), wide(BF16),
                        pltpu.VMEM((2 * blk, LANES), F32), pltpu.VMEM((2 * blk, 1), F32)],
        compiler_params=_params(1),
        name="stick_breaking",
    )(qkv, later)


def kernel(x, l0_ffn1_pre_g, l0_ffn1_post_g, l0_ffn1_w_in, l0_ffn1_w_out, l0_mix_pre_g, l0_mix_post_g, l0_w_in, l0_mu, l0_w0, l0_w2, l0_a0, l0_a2, l0_g2, l0_k_k, l0_k_a, l0_r_k, l0_lnx_g, l0_lnx_b, l0_conv_w, l0_conv_b, l0_gate_a_w, l0_gate_a_b, l0_gate_x_w, l0_gate_x_b, l0_lambda, l0_w_out, l0_ffn2_pre_g, l0_ffn2_post_g, l0_ffn2_w_in, l0_ffn2_w_out, l1_ffn1_pre_g, l1_ffn1_post_g, l1_ffn1_w_in, l1_ffn1_w_out, l1_mix_pre_g, l1_mix_post_g, l1_w_qkv, l1_w_out, l1_ffn2_pre_g, l1_ffn2_post_g, l1_ffn2_w_in, l1_ffn2_w_out):
    b, t, d = x.shape
    a_width = l0_w0.shape[0]
    b_width = l0_lambda.shape[0]
    a_proj = l0_mu.shape[0]
    h = x.reshape(b * t, d)

    h = _ffn(h, l0_ffn1_pre_g, l0_ffn1_post_g, l0_ffn1_w_in, l0_ffn1_w_out)
    pa, xb, gb = _proj(h, l0_mix_pre_g, l0_w_in, (a_proj, b_width, b_width), F32)
    ya, yb = _mixers0(
        (pa.reshape(b, t, a_proj), l0_mu, l0_w0, l0_w2, l0_a0, l0_a2, l0_g2, l0_k_k, l0_k_a,
         l0_r_k.reshape(-1), l0_lnx_g, l0_lnx_b),
        (xb.reshape(b, t, b_width), gb.reshape(b, t, b_width), l0_conv_w, l0_conv_b,
         l0_gate_a_w, l0_gate_a_b, l0_gate_x_w, l0_gate_x_b, l0_lambda))
    h = _ffn(h, l0_ffn2_pre_g, l0_ffn2_post_g, l0_ffn2_w_in, l0_ffn2_w_out,
             mix=([ya.reshape(b * t, a_width), yb.reshape(b * t, b_width)],
                  [l0_w_out[:a_width], l0_w_out[a_width:]], l0_mix_post_g))

    h = _ffn(h, l1_ffn1_pre_g, l1_ffn1_post_g, l1_ffn1_w_in, l1_ffn1_w_out)
    qkv = _proj_tiles(h, l1_mix_pre_g, l1_w_qkv, 3, BF16,
                      scales=(-LOG2E / math.sqrt(HEAD), None, None))
    o = _stick_breaking(qkv, b)
    h = _ffn(h, l1_ffn2_pre_g, l1_ffn2_post_g, l1_ffn2_w_in, l1_ffn2_w_out,
             mix=([o], [l1_w_out], l1_mix_post_g))
    return h.reshape(b, t, d)
```

```python
import functools
import math

import jax
import jax.numpy as jnp
from jax import lax
from jax.experimental import pallas as pl
from jax.experimental.pallas import tpu as pltpu

F32 = jnp.float32
BF16 = jnp.bfloat16

NORM_EPS = 1e-6
GN_EPS = 64e-5
LRU_C = 8.0
HEAD = 64
LANES = 128
SUBLANES = 8
CHUNK = 64
RW_ROWS = 256
CONV_WIDTH = 4
LOG2E = 1.0 / math.log(2.0)
VMEM_LIMIT = 56 * 1024 * 1024


def _params(n_axes, vmem=VMEM_LIMIT):
    return pltpu.CompilerParams(
        dimension_semantics=("arbitrary",) * n_axes, vmem_limit_bytes=vmem)


def _bf(x):
    return x.astype(BF16)


def _dot(a, b):
    return jnp.dot(a, b, preferred_element_type=F32)


def _dot_nt(a, b):
    return lax.dot_general(a, b, (((1,), (1,)), ((), ())), preferred_element_type=F32)


def _dot_hilo(x, e):
    hi = _bf(x)
    lo = _bf(x - hi.astype(F32))
    return _dot(hi, e) + _dot(lo, e)


def _rms(x, g):
    return x * lax.rsqrt(jnp.mean(x * x, axis=-1, keepdims=True) + NORM_EPS) * g


def _sigmoid(x):
    return 1.0 / (1.0 + jnp.exp(-x))


def _softplus(x):
    return jnp.maximum(x, 0.0) + jnp.log1p(jnp.exp(-jnp.abs(x)))


def _gelu_tanh(x):
    c = math.sqrt(2.0 / math.pi)
    return 0.5 * x * (1.0 + jnp.tanh(c * (x + 0.044715 * (x * x * x))))


def _row_spec(tm, n):
    return pl.BlockSpec((tm, n), lambda i, *_: (i, 0))


def _const_spec(shape):
    return pl.BlockSpec(shape, lambda *_: (0,) * len(shape))


def _act_rows(a_ref, rs):
    if len(a_ref.shape) == 2:
        return a_ref[rs, :]
    return jnp.concatenate([a_ref[p, rs, :] for p in range(a_ref.shape[0])], axis=1)


def _ffn_body(*refs, tf, n_mix, n_sub):
    a_refs, w_refs = refs[:n_mix], refs[n_mix:2 * n_mix]
    rest = refs[2 * n_mix:]
    if n_mix:
        gmix_ref, rest = rest[0], rest[1:]
    x_ref, gpre_ref, gpost_ref, win_ref, wo_ref, o_ref = rest
    f = wo_ref.shape[0]
    sub = x_ref.shape[0] // n_sub
    spans = [slice(s * sub, (s + 1) * sub) for s in range(n_sub)]
    xs = [x_ref[rs, :] for rs in spans]
    if n_mix:
        ys = [_dot(_act_rows(a_refs[0], rs), w_refs[0][...]) for rs in spans]
        for a_ref, w_ref in zip(a_refs[1:], w_refs[1:]):
            ys = [y + _dot(_act_rows(a_ref, rs), w_ref[...]) for y, rs in zip(ys, spans)]
        xs = [x + _rms(y, gmix_ref[...]) for x, y in zip(xs, ys)]
    hs = [_bf(_rms(x, gpre_ref[...])) for x in xs]

    def gate_up(h, c):
        return (_dot(h, win_ref[:, c * tf:(c + 1) * tf]),
                _dot(h, win_ref[:, f + c * tf:f + (c + 1) * tf]))

    n = f // tf
    accs = [None] * n_sub
    gus = [gate_up(h, 0) for h in hs]
    for c in range(n):
        for s in range(n_sub):
            gate, up = gus[s]
            act = _bf(gate * _sigmoid(gate) * up)
            if c + 1 < n:
                gus[s] = gate_up(hs[s], c + 1)
            part = _dot(act, wo_ref[c * tf:(c + 1) * tf, :])
            accs[s] = part if accs[s] is None else accs[s] + part
    for rs, x, acc in zip(spans, xs, accs):
        o_ref[rs, :] = x + 0.5 * _rms(acc, gpost_ref[...])


def _ffn(x, g_pre, g_post, w_in, w_out, *, mix=None, tm=1024, n_sub=4, tf=256):
    m, d = x.shape
    tm = min(tm, m)
    f = w_out.shape[0]
    acts, ws, g_mix = mix if mix is not None else ((), (), None)
    resident = lambda shape: pl.BlockSpec(shape, lambda i: (0, 0), pipeline_mode=pl.Buffered(1))
    act_spec = lambda a: (_row_spec(tm, a.shape[1]) if a.ndim == 2 else
                          pl.BlockSpec((a.shape[0], tm, a.shape[2]), lambda i: (0, i, 0)))
    mix_specs = [act_spec(a) for a in acts] + [resident(w.shape) for w in ws]
    mix_args = list(acts) + [_bf(w) for w in ws]
    if mix is not None:
        mix_specs.append(_const_spec((1, d)))
        mix_args.append(g_mix.reshape(1, d))
    return pl.pallas_call(
        functools.partial(_ffn_body, tf=tf, n_mix=len(acts), n_sub=n_sub),
        grid=(m // tm,),
        in_specs=mix_specs + [_row_spec(tm, d), _const_spec((1, d)), _const_spec((1, d)),
                              resident((d, 2 * f)), resident((f, d))],
        out_specs=_row_spec(tm, d),
        out_shape=jax.ShapeDtypeStruct((m, d), F32),
        compiler_params=_params(1),
        name="ffn",
    )(*mix_args, x, g_pre.reshape(1, d), g_post.reshape(1, d), _bf(w_in), _bf(w_out))


def _proj_body(x_ref, g_ref, w_ref, *o_refs, scales):
    h = _bf(_rms(x_ref[...], g_ref[...]))
    y = _dot(h, w_ref[...])
    off = 0
    for o_ref, scale in zip(o_refs, scales):
        n = o_ref.shape[-1]
        part = y[:, off:off + n]
        if scale is not None:
            part = part * scale
        o_ref[...] = part.astype(o_ref.dtype)
        off += n


def _proj(x, g, w, splits, dtype, *, scales=None, tm=512):
    m, d = x.shape
    tm = min(tm, m)
    n = w.shape[1]
    assert sum(splits) == n
    scales = scales or (None,) * len(splits)
    return pl.pallas_call(
        functools.partial(_proj_body, scales=scales),
        grid=(m // tm,),
        in_specs=[_row_spec(tm, d), _const_spec((1, d)), _const_spec((d, n))],
        out_specs=[_row_spec(tm, s) for s in splits],
        out_shape=[jax.ShapeDtypeStruct((m, s), dtype) for s in splits],
        compiler_params=_params(1),
        name="proj",
    )(x, g.reshape(1, d), _bf(w))


def _proj_tiles_body(x_ref, g_ref, w_ref, o_ref, *, scales):
    h = _bf(_rms(x_ref[...], g_ref[...]))
    y = _dot(h, w_ref[...])
    n_groups, n_tiles = o_ref.shape[0], o_ref.shape[1]
    for s in range(n_groups):
        for p in range(n_tiles):
            off = (s * n_tiles + p) * LANES
            part = y[:, off:off + LANES]
            if scales[s] is not None:
                part = part * scales[s]
            o_ref[s, p] = part.astype(o_ref.dtype)


def _proj_tiles(x, g, w, n_groups, dtype, *, scales, tm=512):
    m, d = x.shape
    tm = min(tm, m)
    n = w.shape[1]
    n_tiles = n // (n_groups * LANES)
    return pl.pallas_call(
        functools.partial(_proj_tiles_body, scales=scales),
        grid=(m // tm,),
        in_specs=[_row_spec(tm, d), _const_spec((1, d)), _const_spec((d, n))],
        out_specs=pl.BlockSpec((n_groups, n_tiles, tm, LANES), lambda i: (0, 0, i, 0)),
        out_shape=jax.ShapeDtypeStruct((n_groups, n_tiles, m, LANES), dtype),
        compiler_params=_params(1),
        name="proj_tiles",
    )(x, g.reshape(1, d), _bf(w))


def _solve_unit_lower(ns, xs):
    ps = [_bf(n) for n in ns]
    steps = int(math.log2(CHUNK))
    for i in range(steps):
        xs = [x + _dot(p, _bf(x)) for p, x in zip(ps, xs)]
        if i + 1 < steps:
            ps = [_bf(_dot(p, p)) for p in ps]
    return xs


def _rwkv_body(pa_ref, mu_ref, w0_ref, a0_ref, kk_ref, ka_ref, rk_ref, lng_ref, lnb_ref,
               wwa_ref, g2_ref, tri_ref, hsum_ref, o_ref, prev_scr, s_scr):
    @pl.when(pl.program_id(1) == 0)
    def _():
        prev_scr[...] = jnp.zeros_like(prev_scr)
        s_scr[...] = jnp.zeros_like(s_scr)

    for r0 in range(0, o_ref.shape[1], RW_ROWS):
        _rwkv_rows(r0, pa_ref, mu_ref, w0_ref, a0_ref, kk_ref, ka_ref, rk_ref, lng_ref, lnb_ref,
                   wwa_ref, g2_ref, tri_ref, hsum_ref, o_ref, prev_scr, s_scr)


def _rwkv_rows(r0, pa_ref, mu_ref, w0_ref, a0_ref, kk_ref, ka_ref, rk_ref, lng_ref, lnb_ref,
               wwa_ref, g2_ref, tri_ref, hsum_ref, o_ref, prev_scr, s_scr):
    rows, width = RW_ROWS, o_ref.shape[2]
    n_pairs = width // LANES
    n_chunks = rows // CHUNK
    p = pa_ref[0, r0:r0 + rows]
    row = lax.broadcasted_iota(jnp.int32, p.shape, 0)
    shifted = jnp.where(row == 0, prev_scr[...], pltpu.roll(p, 1, axis=0))
    prev_scr[...] = p[rows - 1:rows, :]
    xm = p + mu_ref[...] * (shifted - p)

    r = xm[:, 0:width]
    k = xm[:, width:2 * width]
    v = xm[:, 2 * width:3 * width]
    wa_lo = xm[:, 3 * width:3 * width + LANES]
    g_lo = xm[:, 3 * width + LANES:]
    lane_wa = lax.broadcasted_iota(jnp.int32, wa_lo.shape, 1)
    wa_act = jnp.where(lane_wa < LANES // 2, jnp.tanh(wa_lo), wa_lo)
    lora = _dot(_bf(wa_act), wwa_ref[...])
    w_log = -_softplus(-(w0_ref[...] + lora[:, :width])) - 0.5
    log_decay = -jnp.exp(w_log)
    a = _sigmoid(a0_ref[...] + lora[:, width:])
    gate = _dot(_bf(_sigmoid(g_lo)), g2_ref[...])

    hsum = hsum_ref[...]
    kk = k * kk_ref[...]
    kk = kk / jnp.maximum(jnp.sqrt(_dot_hilo(kk * kk, hsum)), 1e-12)
    k_h = k * (1.0 + (a - 1.0) * ka_ref[...])

    ld_hi = _bf(log_decay)
    ld_lo = _bf(log_decay - ld_hi.astype(F32))
    tri = tri_ref[...]
    cum = _dot(tri, ld_hi) + _dot(tri, ld_lo)
    dec_in = jnp.exp(cum)
    dec_out = jnp.exp(-cum)
    r_f = r * dec_in
    a_f = -kk * jnp.exp(cum - log_decay)
    b_t = _bf(kk * a * dec_out)
    k_t = _bf(k_h * dec_out)

    lane = lax.broadcasted_iota(jnp.int32, (CHUNK, LANES), 1)
    trow = lax.broadcasted_iota(jnp.int32, (CHUNK, LANES), 0)
    first = lane < HEAD
    src = jnp.where(first, lane, lane - HEAD)
    strict = src < trow
    incl = src <= trow
    lane2 = lax.broadcasted_iota(jnp.int32, (CHUNK, 2 * LANES), 1)
    first2 = jnp.where(lane2 < LANES, lane2, lane2 - LANES) < HEAD
    sr = lax.broadcasted_iota(jnp.int32, (LANES, LANES), 0)
    sc = lax.broadcasted_iota(jnp.int32, (LANES, LANES), 1)
    same_head = (sr < HEAD) == (sc < HEAD)
    z16 = jnp.zeros((CHUNK, LANES), BF16)
    z32 = jnp.zeros((CHUNK, LANES), F32)

    inst = [(c, pr) for c in range(n_chunks) for pr in range(n_pairs)]
    blk = lambda z, c, pr: z[c * CHUNK:(c + 1) * CHUNK, pr * LANES:(pr + 1) * LANES]
    afs = [blk(a_f, c, pr) for c, pr in inst]
    rfs = [blk(r_f, c, pr) for c, pr in inst]
    vfs = [blk(v, c, pr) for c, pr in inst]
    vts = [_bf(vf) for vf in vfs]
    bks = [jnp.concatenate([blk(b_t, c, pr), blk(k_t, c, pr)], axis=0) for c, pr in inst]
    kbs = [jnp.concatenate([blk(k_t, c, pr), blk(b_t, c, pr)], axis=0) for c, pr in inst]
    gh0s = [_dot_nt(jnp.concatenate([jnp.where(first, _bf(af), 0), jnp.where(first, _bf(rf), 0)],
                                    axis=0), bk) for af, rf, bk in zip(afs, rfs, bks)]
    gh1s = [_dot_nt(jnp.concatenate([jnp.where(first, 0, _bf(af)), jnp.where(first, 0, _bf(rf))],
                                    axis=0), kb) for af, rf, kb in zip(afs, rfs, kbs)]
    g0s = [jnp.where(strict, gh[:CHUNK], 0.0) for gh in gh0s]
    g1s = [jnp.where(strict, gh[:CHUNK], 0.0) for gh in gh1s]
    lhss = [jnp.concatenate(
        [jnp.concatenate([_bf(jnp.where(incl, ga[CHUNK:], 0.0)), z16], axis=1),
         jnp.concatenate([z16, _bf(jnp.where(incl, gb[CHUNK:], 0.0))], axis=1)], axis=0)
        for ga, gb in zip(gh0s, gh1s)]
    n_blks = [jnp.concatenate([jnp.where(first, g0, 0.0), jnp.where(first, 0.0, g1)], axis=0)
              for g0, g1 in zip(g0s, g1s)]
    akvs = [_dot(_bf(jnp.concatenate([jnp.where(first, 0.0, g0), jnp.where(first, g1, 0.0)],
                                     axis=0)), jnp.concatenate([vt, vt], axis=0))
            for g0, g1, vt in zip(g0s, g1s, vts)]
    xs = [jnp.concatenate([jnp.concatenate([af, af], axis=0), akv], axis=1)
          for af, akv in zip(afs, akvs)]
    xs = _solve_unit_lower(n_blks, xs)
    lows = [jnp.concatenate([jnp.where(first2, x[:CHUNK], x[CHUNK:]),
                             jnp.concatenate([z32, vf], axis=1)], axis=0)
            for x, vf in zip(xs, vfs)]
    lowbs = [_bf(low) for low in lows]
    outs = [_dot(lhs, jnp.concatenate([lowb, lowb[CHUNK:], lowb[:CHUNK]], axis=0))
            for lhs, lowb in zip(lhss, lowbs)]
    sels = [jnp.where(first2, out[:CHUNK], out[CHUNK:]) for out in outs]
    mcs = [_dot(_bf(low.T), bk) for low, bk in zip(lows, bks)]

    states = [s_scr[pr] for pr in range(n_pairs)]
    y_rows = []
    for c in range(n_chunks):
        y_pairs = []
        for pr in range(n_pairs):
            i = c * n_pairs + pr
            w_last = dec_in[(c + 1) * CHUNK - 1:(c + 1) * CHUNK, pr * LANES:(pr + 1) * LANES]
            g_tilde = rfs[i] + sels[i][:, :LANES]
            y_intra = sels[i][:, LANES:]
            m_lr = jnp.where(same_head, mcs[i][:LANES], 0.0) * w_last
            c_mat = jnp.where(same_head, mcs[i][LANES:], 0.0) * w_last
            s = states[pr]
            sb = _bf(s)
            y_pairs.append(_dot_nt(_bf(g_tilde), sb) + y_intra)
            states[pr] = s * w_last + _dot(sb, _bf(m_lr)) + c_mat
        y_rows.append(jnp.concatenate(y_pairs, axis=-1))
    for pr in range(n_pairs):
        s_scr[pr] = states[pr]
    y = jnp.concatenate(y_rows, axis=0)

    inv_n = 1.0 / HEAD
    mean = _dot_hilo(y, hsum) * inv_n
    dev = y - mean
    var = _dot_hilo(dev * dev, hsum) * inv_n
    yn = dev * lax.rsqrt(var + GN_EPS) * lng_ref[...] + lnb_ref[...]
    bonus = _dot_hilo(r * k_h * rk_ref[...], hsum) * v
    o_ref[0, r0:r0 + rows] = ((yn + bonus) * gate).astype(o_ref.dtype)


def _rwkv(pa, mu, w0, w2, a0, a2, g2, k_k, k_a, r_k, lnx_g, lnx_b, *, rows=RW_ROWS):
    b, t, ap = pa.shape
    rows = min(rows, t)
    assert rows % RW_ROWS == 0
    width = w0.shape[0]
    w_lora, a_lora = w2.shape[0], a2.shape[0]
    assert w_lora == HEAD and a_lora == HEAD and width % LANES == 0
    n_pairs = width // LANES
    wwa = jnp.zeros((w_lora + a_lora, 2 * width), F32)
    wwa = wwa.at[:w_lora, :width].set(w2).at[w_lora:, width:].set(a2)
    ti = jnp.arange(RW_ROWS)
    tri = ((ti[None, :] <= ti[:, None])
           & (ti[None, :] // CHUNK == ti[:, None] // CHUNK)).astype(BF16)
    hi = jnp.arange(width) // HEAD
    hsum = (hi[:, None] == hi[None, :]).astype(BF16)
    vec = lambda z: z.reshape(1, width)
    return pl.pallas_call(
        _rwkv_body,
        grid=(b, t // rows),
        in_specs=[
            pl.BlockSpec((1, rows, ap), lambda i, c: (i, c, 0)),
            _const_spec((1, ap)),
            _const_spec((1, width)), _const_spec((1, width)), _const_spec((1, width)),
            _const_spec((1, width)), _const_spec((1, width)), _const_spec((1, width)),
            _const_spec((1, width)),
            _const_spec((w_lora + a_lora, 2 * width)),
            _const_spec(g2.shape),
            _const_spec((RW_ROWS, RW_ROWS)),
            _const_spec((width, width)),
        ],
        out_specs=pl.BlockSpec((1, rows, width), lambda i, c: (i, c, 0)),
        out_shape=jax.ShapeDtypeStruct((b, t, width), BF16),
        scratch_shapes=[pltpu.VMEM((1, ap), F32), pltpu.VMEM((n_pairs, LANES, LANES), F32)],
        compiler_params=_params(2),
        name="rwkv7",
    )(pa, mu.reshape(1, ap), vec(w0), vec(a0), vec(k_k), vec(k_a), vec(r_k), vec(lnx_g),
      vec(lnx_b), _bf(wwa), _bf(g2), tri, hsum)


def _rglru_body(xb_ref, gb_ref, cw_ref, cb_ref, wg_ref, bg_ref, lam_ref, o_ref,
                prev_scr, h_scr):
    i = pl.program_id(1)
    tt, width = o_ref.shape[1], o_ref.shape[2]

    @pl.when(i == 0)
    def _():
        prev_scr[...] = jnp.zeros_like(prev_scr)
        h_scr[...] = jnp.zeros_like(h_scr)

    x = xb_ref[0]
    prev = prev_scr[...]
    prev_scr[...] = x[tt - 8:, :]
    row8 = lax.broadcasted_iota(jnp.int32, (8, width), 0)

    def delayed(d):
        rolled = pltpu.roll(x, d, axis=0)
        head = jnp.where(row8 < d, pltpu.roll(prev, d, axis=0), rolled[:8])
        return jnp.concatenate([head, rolled[8:]], axis=0)

    xc = cb_ref[...] + cw_ref[CONV_WIDTH - 1:CONV_WIDTH, :] * x
    for d in range(1, CONV_WIDTH):
        xc = xc + cw_ref[CONV_WIDTH - 1 - d:CONV_WIDTH - d, :] * delayed(d)

    gates = _dot(_bf(xc), wg_ref[...]) + bg_ref[...]
    r_g = _sigmoid(gates[:, :width])
    i_g = _sigmoid(gates[:, width:])
    log_a = -LRU_C * r_g * _softplus(-lam_ref[...])
    a = jnp.exp(log_a)
    mult = jnp.sqrt(jnp.tanh(-log_a) * (1.0 + a * a))
    u = mult * i_g * xc

    a = a.reshape(tt // SUBLANES, SUBLANES, width)
    u = u.reshape(tt // SUBLANES, SUBLANES, width)
    sub = lax.broadcasted_iota(jnp.int32, a.shape, 1)
    d = 1
    while d < SUBLANES:
        keep = sub >= d
        a_sh = jnp.where(keep, pltpu.roll(a, d, axis=1), 1.0)
        u_sh = jnp.where(keep, pltpu.roll(u, d, axis=1), 0.0)
        u = a * u_sh + u
        a = a * a_sh
        d *= 2
    carry = h_scr[...]
    groups = []
    for g in range(tt // SUBLANES):
        hg = u[g] + a[g] * carry
        groups.append(hg)
        carry = hg[SUBLANES - 1:]
    h = jnp.concatenate(groups, axis=0)
    h_scr[...] = carry
    o_ref[0] = (h * _gelu_tanh(gb_ref[0])).astype(o_ref.dtype)


def _rglru(xb, gb, conv_w, conv_b, gate_a_w, gate_a_b, gate_x_w, gate_x_b, lam, *, tt=256):
    b, t, width = xb.shape
    tt = min(tt, t)
    nb, bd, _ = gate_a_w.shape
    eye = jnp.eye(nb, dtype=F32)
    dense = lambda w: jnp.einsum("nij,nm->nimj", w, eye).reshape(width, width)
    wg = _bf(jnp.concatenate([dense(gate_a_w), dense(gate_x_w)], axis=1))
    bg = jnp.concatenate([gate_a_b, gate_x_b]).reshape(1, 2 * width)
    tile = pl.BlockSpec((1, tt, width), lambda i, j: (i, j, 0))
    return pl.pallas_call(
        _rglru_body,
        grid=(b, t // tt),
        in_specs=[tile, tile, _const_spec((CONV_WIDTH, width)), _const_spec((1, width)),
                  _const_spec((width, 2 * width)), _const_spec((1, 2 * width)),
                  _const_spec((1, width))],
        out_specs=tile,
        out_shape=jax.ShapeDtypeStruct((b, t, width), BF16),
        scratch_shapes=[pltpu.VMEM((8, width), F32), pltpu.VMEM((1, width), F32)],
        compiler_params=_params(2),
        name="rglru",
    )(xb, gb, conv_w, conv_b.reshape(1, width), wg, bg, lam.reshape(1, width))


SB_DEPTH = 5
SB_ROWS = 64


def _sb_body(qkv_ref, later_ref, o_ref, zn_scr, cs_scr, lsig_scr, negb_scr, att_scr, acc_scr,
             run_scr, *, blk):
    n_p = qkv_ref.shape[1]
    n_q = qkv_ref.shape[2] // blk
    n_steps = n_p * n_q * (n_q + 1) // 2 + SB_DEPTH - 1
    first = lax.broadcasted_iota(jnp.int32, (blk, LANES), 1) < HEAD
    later = later_ref[...]
    rr = lax.broadcasted_iota(jnp.int32, (2 * blk, blk), 0)
    cc = lax.broadcasted_iota(jnp.int32, (2 * blk, blk), 1)
    ahead = cc - jnp.where(rr < blk, rr, rr - blk)

    def rows(which, p, i):
        return qkv_ref[which, p, pl.ds(pl.multiple_of(i * blk, blk), blk), :]

    def step(cur, triples):
        old = 1 - cur
        ps, qis, js = triples
        p0 = jnp.minimum(ps[0], n_p - 1)
        qa = rows(0, p0, jnp.minimum(qis[0], n_q - 1))
        q2 = jnp.concatenate([jnp.where(first, qa, 0), jnp.where(first, 0, qa)], axis=0)
        zn_scr[cur] = _dot_nt(q2, rows(1, p0, jnp.minimum(js[0], n_q - 1)))
        cs_scr[cur] = _dot(negb_scr[old], later)
        pv = _dot(att_scr[old], rows(2, ps[4], js[4]))
        for r in range(0, 2 * blk, SB_ROWS):
            rs = slice(r, r + SB_ROWS)
            att_scr[cur, rs] = _bf(jnp.exp2(lsig_scr[cur, rs] + cs_scr[old, rs]))
        for r in range(0, 2 * blk, SB_ROWS):
            rs = slice(r, r + SB_ROWS)
            znm = jnp.where(ahead[rs] < (qis[1] - js[1]) * blk, zn_scr[old, rs], 1e30)
            low = jnp.minimum(znm, 0.0)
            top = low - znm
            lse = jnp.log(1.0 + jnp.exp2(low + top)) * LOG2E
            neg = low - lse
            run = jnp.where(js[1] == qis[1], 0.0, run_scr[rs])
            negb_scr[cur, rs] = _bf(neg)
            lsig_scr[cur, rs] = (top - lse) + run
            run_scr[rs] = run + jnp.sum(neg, axis=-1, keepdims=True)
        acc = jnp.where(js[4] == qis[4], 0.0, acc_scr[...]) + pv
        acc_scr[...] = acc
        o_ref[ps[4], pl.ds(pl.multiple_of(qis[4] * blk, blk), blk), :] = jnp.where(
            first, acc[:blk], acc[blk:]).astype(o_ref.dtype)
        blk_done = js[0] == 0
        pair_done = jnp.logical_and(blk_done, qis[0] == n_q - 1)
        q_new = jnp.where(pair_done, 0, qis[0] + 1)
        p_next = jnp.where(pair_done, ps[0] + 1, ps[0])
        q_next = jnp.where(blk_done, q_new, qis[0])
        j_next = jnp.where(blk_done, q_new, js[0] - 1)
        return (p_next,) + ps[:-1], (q_next,) + qis[:-1], (j_next,) + js[:-1]

    for scr in (zn_scr, cs_scr, lsig_scr, negb_scr, att_scr, acc_scr, run_scr):
        scr[...] = jnp.zeros_like(scr)
    triples = ((jnp.int32(0),) * SB_DEPTH,) * 3
    if n_steps % 2:
        triples = step(1, triples)
    lax.fori_loop(0, n_steps // 2, lambda _, tr: step(1, step(0, tr)), triples)


def _stick_breaking(qkv, batch, *, blk=256):
    _, n_p, m, _ = qkv.shape
    t = m // batch
    blk = min(blk, t)
    ki = jnp.arange(blk)
    later = (ki[:, None] > ki[None, :]).astype(BF16)
    wide = lambda dt: pltpu.VMEM((2, 2 * blk, blk), dt)
    return pl.pallas_call(
        functools.partial(_sb_body, blk=blk),
        grid=(batch,),
        in_specs=[pl.BlockSpec((3, n_p, t, LANES), lambda i: (0, 0, i, 0)),
                  _const_spec((blk, blk))],
        out_specs=pl.BlockSpec((n_p, t, LANES), lambda i: (0, i, 0)),
        out_shape=jax.ShapeDtypeStruct((n_p, m, LANES), BF16),
        scratch_shapes=[wide(F32), wide(F32), wide(F32), wide(BF16), wide(BF16),
                        pltpu.VMEM((2 * blk, LANES), F32), pltpu.VMEM((2 * blk, 1), F32)],
        compiler_params=_params(1),
        name="stick_breaking",
    )(qkv, later)


def kernel(x, l0_ffn1_pre_g, l0_ffn1_post_g, l0_ffn1_w_in, l0_ffn1_w_out, l0_mix_pre_g, l0_mix_post_g, l0_w_in, l0_mu, l0_w0, l0_w2, l0_a0, l0_a2, l0_g2, l0_k_k, l0_k_a, l0_r_k, l0_lnx_g, l0_lnx_b, l0_conv_w, l0_conv_b, l0_gate_a_w, l0_gate_a_b, l0_gate_x_w, l0_gate_x_b, l0_lambda, l0_w_out, l0_ffn2_pre_g, l0_ffn2_post_g, l0_ffn2_w_in, l0_ffn2_w_out, l1_ffn1_pre_g, l1_ffn1_post_g, l1_ffn1_w_in, l1_ffn1_w_out, l1_mix_pre_g, l1_mix_post_g, l1_w_qkv, l1_w_out, l1_ffn2_pre_g, l1_ffn2_post_g, l1_ffn2_w_in, l1_ffn2_w_out):
    b, t, d = x.shape
    a_width = l0_w0.shape[0]
    b_width = l0_lambda.shape[0]
    a_proj = l0_mu.shape[0]
    h = x.reshape(b * t, d)

    h = _ffn(h, l0_ffn1_pre_g, l0_ffn1_post_g, l0_ffn1_w_in, l0_ffn1_w_out)
    pa, xb, gb = _proj(h, l0_mix_pre_g, l0_w_in, (a_proj, b_width, b_width), F32)
    ya = _rwkv(pa.reshape(b, t, a_proj), l0_mu, l0_w0, l0_w2, l0_a0, l0_a2, l0_g2, l0_k_k,
               l0_k_a, l0_r_k.reshape(-1), l0_lnx_g, l0_lnx_b)
    yb = _rglru(xb.reshape(b, t, b_width), gb.reshape(b, t, b_width), l0_conv_w, l0_conv_b,
                l0_gate_a_w, l0_gate_a_b, l0_gate_x_w, l0_gate_x_b, l0_lambda)
    h = _ffn(h, l0_ffn2_pre_g, l0_ffn2_post_g, l0_ffn2_w_in, l0_ffn2_w_out,
             mix=([ya.reshape(b * t, a_width), yb.reshape(b * t, b_width)],
                  [l0_w_out[:a_width], l0_w_out[a_width:]], l0_mix_post_g))

    h = _ffn(h, l1_ffn1_pre_g, l1_ffn1_post_g, l1_ffn1_w_in, l1_ffn1_w_out)
    qkv = _proj_tiles(h, l1_mix_pre_g, l1_w_qkv, 3, BF16,
                      scales=(-LOG2E / math.sqrt(HEAD), None, None))
    o = _stick_breaking(qkv, b)
    h = _ffn(h, l1_ffn2_pre_g, l1_ffn2_post_g, l1_ffn2_w_in, l1_ffn2_w_out,
             mix=([o], [l1_w_out], l1_mix_post_g))
    return h.reshape(b, t, d)
```

```python
import functools
import math

import jax
import jax.numpy as jnp
from jax import lax
from jax.experimental import pallas as pl
from jax.experimental.pallas import tpu as pltpu

F32 = jnp.float32
BF16 = jnp.bfloat16

NORM_EPS = 1e-6
GN_EPS = 64e-5
LRU_C = 8.0
HEAD = 64
LANES = 128
SUBLANES = 8
CHUNK = 64
RW_ROWS = 256
CONV_WIDTH = 4
LOG2E = 1.0 / math.log(2.0)
VMEM_LIMIT = 56 * 1024 * 1024


def _params(n_axes, vmem=VMEM_LIMIT):
    return pltpu.CompilerParams(
        dimension_semantics=("arbitrary",) * n_axes, vmem_limit_bytes=vmem)


def _bf(x):
    return x.astype(BF16)


def _dot(a, b):
    return jnp.dot(a, b, preferred_element_type=F32)


def _dot_nt(a, b):
    return lax.dot_general(a, b, (((1,), (1,)), ((), ())), preferred_element_type=F32)


def _dot_hilo(x, e):
    hi = _bf(x)
    lo = _bf(x - hi.astype(F32))
    return _dot(hi, e) + _dot(lo, e)


def _rms(x, g):
    return x * lax.rsqrt(jnp.mean(x * x, axis=-1, keepdims=True) + NORM_EPS) * g


def _sigmoid(x):
    return 1.0 / (1.0 + jnp.exp(-x))


def _softplus(x):
    return jnp.maximum(x, 0.0) + jnp.log1p(jnp.exp(-jnp.abs(x)))


def _gelu_tanh(x):
    c = math.sqrt(2.0 / math.pi)
    return 0.5 * x * (1.0 + jnp.tanh(c * (x + 0.044715 * (x * x * x))))


def _row_spec(tm, n):
    return pl.BlockSpec((tm, n), lambda i, *_: (i, 0))


def _const_spec(shape):
    return pl.BlockSpec(shape, lambda *_: (0,) * len(shape))


def _act_rows(a_ref, rs):
    if len(a_ref.shape) == 2:
        return a_ref[rs, :]
    return jnp.concatenate([a_ref[p, rs, :] for p in range(a_ref.shape[0])], axis=1)


def _ffn_body(*refs, tf, n_mix, n_sub):
    a_refs, rest = refs[:n_mix], refs[n_mix:]
    if n_mix:
        wmix_ref, gmix_ref, rest = rest[0], rest[1], rest[2:]
    x_ref, gpre_ref, gpost_ref, win_ref, wo_ref, o_ref = rest
    f = wo_ref.shape[0]
    sub = x_ref.shape[0] // n_sub
    spans = [slice(s * sub, (s + 1) * sub) for s in range(n_sub)]
    xs = [x_ref[rs, :] for rs in spans]
    if n_mix:
        ys = [_dot(jnp.concatenate([_act_rows(a_ref, rs) for a_ref in a_refs], axis=1),
                   wmix_ref[...]) for rs in spans]
        xs = [x + _rms(y, gmix_ref[...]) for x, y in zip(xs, ys)]
    hs = [_bf(_rms(x, gpre_ref[...])) for x in xs]

    def gate_up(h, c):
        return (_dot(h, win_ref[:, c * tf:(c + 1) * tf]),
                _dot(h, win_ref[:, f + c * tf:f + (c + 1) * tf]))

    n = f // tf
    accs = [None] * n_sub
    gus = [gate_up(h, 0) for h in hs]
    for c in range(n):
        for s in range(n_sub):
            gate, up = gus[s]
            act = _bf(gate * _sigmoid(gate) * up)
            if c + 1 < n:
                gus[s] = gate_up(hs[s], c + 1)
            part = _dot(act, wo_ref[c * tf:(c + 1) * tf, :])
            accs[s] = part if accs[s] is None else accs[s] + part
    for rs, x, acc in zip(spans, xs, accs):
        o_ref[rs, :] = x + 0.5 * _rms(acc, gpost_ref[...])


def _ffn(x, g_pre, g_post, w_in, w_out, *, mix=None, tm=1024, n_sub=4, tf=256):
    m, d = x.shape
    tm = min(tm, m)
    f = w_out.shape[0]
    acts, w_mix, g_mix = mix if mix is not None else ((), None, None)
    resident = lambda shape: pl.BlockSpec(shape, lambda i: (0, 0), pipeline_mode=pl.Buffered(1))
    act_spec = lambda a: (_row_spec(tm, a.shape[1]) if a.ndim == 2 else
                          pl.BlockSpec((a.shape[0], tm, a.shape[2]), lambda i: (0, i, 0)))
    mix_specs = [act_spec(a) for a in acts]
    mix_args = list(acts)
    if mix is not None:
        mix_specs += [resident(w_mix.shape), _const_spec((1, d))]
        mix_args += [_bf(w_mix), g_mix.reshape(1, d)]
    return pl.pallas_call(
        functools.partial(_ffn_body, tf=tf, n_mix=len(acts), n_sub=n_sub),
        grid=(m // tm,),
        in_specs=mix_specs + [_row_spec(tm, d), _const_spec((1, d)), _const_spec((1, d)),
                              resident((d, 2 * f)), resident((f, d))],
        out_specs=_row_spec(tm, d),
        out_shape=jax.ShapeDtypeStruct((m, d), F32),
        compiler_params=_params(1),
        name="ffn",
    )(*mix_args, x, g_pre.reshape(1, d), g_post.reshape(1, d), _bf(w_in), _bf(w_out))


def _proj_body(x_ref, g_ref, w_ref, *o_refs, scales):
    h = _bf(_rms(x_ref[...], g_ref[...]))
    y = _dot(h, w_ref[...])
    off = 0
    for o_ref, scale in zip(o_refs, scales):
        n = o_ref.shape[-1]
        part = y[:, off:off + n]
        if scale is not None:
            part = part * scale
        o_ref[...] = part.astype(o_ref.dtype)
        off += n


def _proj(x, g, w, splits, dtype, *, scales=None, tm=512):
    m, d = x.shape
    tm = min(tm, m)
    n = w.shape[1]
    assert sum(splits) == n
    scales = scales or (None,) * len(splits)
    return pl.pallas_call(
        functools.partial(_proj_body, scales=scales),
        grid=(m // tm,),
        in_specs=[_row_spec(tm, d), _const_spec((1, d)), _const_spec((d, n))],
        out_specs=[_row_spec(tm, s) for s in splits],
        out_shape=[jax.ShapeDtypeStruct((m, s), dtype) for s in splits],
        compiler_params=_params(1),
        name="proj",
    )(x, g.reshape(1, d), _bf(w))


def _proj_tiles_body(x_ref, g_ref, w_ref, o_ref, *, scales):
    h = _bf(_rms(x_ref[...], g_ref[...]))
    y = _dot(h, w_ref[...])
    n_groups, n_tiles = o_ref.shape[0], o_ref.shape[1]
    for s in range(n_groups):
        for p in range(n_tiles):
            off = (s * n_tiles + p) * LANES
            part = y[:, off:off + LANES]
            if scales[s] is not None:
                part = part * scales[s]
            o_ref[s, p] = part.astype(o_ref.dtype)


def _proj_tiles(x, g, w, n_groups, dtype, *, scales, tm=512):
    m, d = x.shape
    tm = min(tm, m)
    n = w.shape[1]
    n_tiles = n // (n_groups * LANES)
    return pl.pallas_call(
        functools.partial(_proj_tiles_body, scales=scales),
        grid=(m // tm,),
        in_specs=[_row_spec(tm, d), _const_spec((1, d)), _const_spec((d, n))],
        out_specs=pl.BlockSpec((n_groups, n_tiles, tm, LANES), lambda i: (0, 0, i, 0)),
        out_shape=jax.ShapeDtypeStruct((n_groups, n_tiles, m, LANES), dtype),
        compiler_params=_params(1),
        name="proj_tiles",
    )(x, g.reshape(1, d), _bf(w))


def _solve_unit_lower(ns, xs):
    ps = [_bf(n) for n in ns]
    steps = int(math.log2(CHUNK))
    for i in range(steps):
        xs = [x + _dot(p, _bf(x)) for p, x in zip(ps, xs)]
        if i + 1 < steps:
            ps = [_bf(_dot(p, p)) for p in ps]
    return xs


def _rwkv_body(pa_ref, mu_ref, w0_ref, a0_ref, kk_ref, ka_ref, rk_ref, lng_ref, lnb_ref,
               wwa_ref, g2_ref, tri_ref, hsum_ref, o_ref, prev_scr, s_scr):
    @pl.when(pl.program_id(1) == 0)
    def _():
        prev_scr[...] = jnp.zeros_like(prev_scr)
        s_scr[...] = jnp.zeros_like(s_scr)

    for r0 in range(0, o_ref.shape[1], RW_ROWS):
        _rwkv_rows(r0, pa_ref, mu_ref, w0_ref, a0_ref, kk_ref, ka_ref, rk_ref, lng_ref, lnb_ref,
                   wwa_ref, g2_ref, tri_ref, hsum_ref, o_ref, prev_scr, s_scr)


def _rwkv_rows(r0, pa_ref, mu_ref, w0_ref, a0_ref, kk_ref, ka_ref, rk_ref, lng_ref, lnb_ref,
               wwa_ref, g2_ref, tri_ref, hsum_ref, o_ref, prev_scr, s_scr):
    rows, width = RW_ROWS, o_ref.shape[2]
    n_pairs = width // LANES
    n_chunks = rows // CHUNK
    p = pa_ref[0, r0:r0 + rows]
    row = lax.broadcasted_iota(jnp.int32, p.shape, 0)
    shifted = jnp.where(row == 0, prev_scr[...], pltpu.roll(p, 1, axis=0))
    prev_scr[...] = p[rows - 1:rows, :]
    xm = p + mu_ref[...] * (shifted - p)

    r = xm[:, 0:width]
    k = xm[:, width:2 * width]
    v = xm[:, 2 * width:3 * width]
    wa_lo = xm[:, 3 * width:3 * width + LANES]
    g_lo = xm[:, 3 * width + LANES:]
    lane_wa = lax.broadcasted_iota(jnp.int32, wa_lo.shape, 1)
    wa_act = jnp.where(lane_wa < LANES // 2, jnp.tanh(wa_lo), wa_lo)
    lora = _dot(_bf(wa_act), wwa_ref[...])
    w_log = -_softplus(-(w0_ref[...] + lora[:, :width])) - 0.5
    log_decay = -jnp.exp(w_log)
    a = _sigmoid(a0_ref[...] + lora[:, width:])
    gate = _dot(_bf(_sigmoid(g_lo)), g2_ref[...])

    hsum = hsum_ref[...]
    kk = k * kk_ref[...]
    kk = kk / jnp.maximum(jnp.sqrt(_dot_hilo(kk * kk, hsum)), 1e-12)
    k_h = k * (1.0 + (a - 1.0) * ka_ref[...])

    ld_hi = _bf(log_decay)
    ld_lo = _bf(log_decay - ld_hi.astype(F32))
    tri = tri_ref[...]
    cum = _dot(tri, ld_hi) + _dot(tri, ld_lo)
    dec_in = jnp.exp(cum)
    dec_out = jnp.exp(-cum)
    r_f = r * dec_in
    a_f = -kk * jnp.exp(cum - log_decay)
    b_t = _bf(kk * a * dec_out)
    k_t = _bf(k_h * dec_out)

    lane = lax.broadcasted_iota(jnp.int32, (CHUNK, LANES), 1)
    trow = lax.broadcasted_iota(jnp.int32, (CHUNK, LANES), 0)
    first = lane < HEAD
    src = jnp.where(first, lane, lane - HEAD)
    strict = src < trow
    incl = src <= trow
    lane2 = lax.broadcasted_iota(jnp.int32, (CHUNK, 2 * LANES), 1)
    first2 = jnp.where(lane2 < LANES, lane2, lane2 - LANES) < HEAD
    sr = lax.broadcasted_iota(jnp.int32, (LANES, LANES), 0)
    sc = lax.broadcasted_iota(jnp.int32, (LANES, LANES), 1)
    same_head = (sr < HEAD) == (sc < HEAD)
    z16 = jnp.zeros((CHUNK, LANES), BF16)
    z32 = jnp.zeros((CHUNK, LANES), F32)

    inst = [(c, pr) for c in range(n_chunks) for pr in range(n_pairs)]
    blk = lambda z, c, pr: z[c * CHUNK:(c + 1) * CHUNK, pr * LANES:(pr + 1) * LANES]
    afs = [blk(a_f, c, pr) for c, pr in inst]
    rfs = [blk(r_f, c, pr) for c, pr in inst]
    vfs = [blk(v, c, pr) for c, pr in inst]
    vts = [_bf(vf) for vf in vfs]
    bks = [jnp.concatenate([blk(b_t, c, pr), blk(k_t, c, pr)], axis=0) for c, pr in inst]
    kbs = [jnp.concatenate([blk(k_t, c, pr), blk(b_t, c, pr)], axis=0) for c, pr in inst]
    gh0s = [_dot_nt(jnp.concatenate([jnp.where(first, _bf(af), 0), jnp.where(first, _bf(rf), 0)],
                                    axis=0), bk) for af, rf, bk in zip(afs, rfs, bks)]
    gh1s = [_dot_nt(jnp.concatenate([jnp.where(first, 0, _bf(af)), jnp.where(first, 0, _bf(rf))],
                                    axis=0), kb) for af, rf, kb in zip(afs, rfs, kbs)]
    g0s = [jnp.where(strict, gh[:CHUNK], 0.0) for gh in gh0s]
    g1s = [jnp.where(strict, gh[:CHUNK], 0.0) for gh in gh1s]
    lhss = [jnp.concatenate(
        [jnp.concatenate([_bf(jnp.where(incl, ga[CHUNK:], 0.0)), z16], axis=1),
         jnp.concatenate([z16, _bf(jnp.where(incl, gb[CHUNK:], 0.0))], axis=1)], axis=0)
        for ga, gb in zip(gh0s, gh1s)]
    n_blks = [jnp.concatenate([jnp.where(first, g0, 0.0), jnp.where(first, 0.0, g1)], axis=0)
              for g0, g1 in zip(g0s, g1s)]
    akvs = [_dot(_bf(jnp.concatenate([jnp.where(first, 0.0, g0), jnp.where(first, g1, 0.0)],
                                     axis=0)), jnp.concatenate([vt, vt], axis=0))
            for g0, g1, vt in zip(g0s, g1s, vts)]
    xs = [jnp.concatenate([jnp.concatenate([af, af], axis=0), akv], axis=1)
          for af, akv in zip(afs, akvs)]
    xs = _solve_unit_lower(n_blks, xs)
    lows = [jnp.concatenate([jnp.where(first2, x[:CHUNK], x[CHUNK:]),
                             jnp.concatenate([z32, vf], axis=1)], axis=0)
            for x, vf in zip(xs, vfs)]
    lowbs = [_bf(low) for low in lows]
    outs = [_dot(lhs, jnp.concatenate([lowb, lowb[CHUNK:], lowb[:CHUNK]], axis=0))
            for lhs, lowb in zip(lhss, lowbs)]
    sels = [jnp.where(first2, out[:CHUNK], out[CHUNK:]) for out in outs]
    mcs = [_dot(_bf(low.T), bk) for low, bk in zip(lows, bks)]

    states = [s_scr[pr] for pr in range(n_pairs)]
    y_rows = []
    for c in range(n_chunks):
        y_pairs = []
        for pr in range(n_pairs):
            i = c * n_pairs + pr
            w_last = dec_in[(c + 1) * CHUNK - 1:(c + 1) * CHUNK, pr * LANES:(pr + 1) * LANES]
            g_tilde = rfs[i] + sels[i][:, :LANES]
            y_intra = sels[i][:, LANES:]
            m_lr = jnp.where(same_head, mcs[i][:LANES], 0.0) * w_last
            c_mat = jnp.where(same_head, mcs[i][LANES:], 0.0) * w_last
            s = states[pr]
            sb = _bf(s)
            y_pairs.append(_dot_nt(_bf(g_tilde), sb) + y_intra)
            states[pr] = s * w_last + _dot(sb, _bf(m_lr)) + c_mat
        y_rows.append(jnp.concatenate(y_pairs, axis=-1))
    for pr in range(n_pairs):
        s_scr[pr] = states[pr]
    y = jnp.concatenate(y_rows, axis=0)

    inv_n = 1.0 / HEAD
    mean = _dot_hilo(y, hsum) * inv_n
    dev = y - mean
    var = _dot_hilo(dev * dev, hsum) * inv_n
    yn = dev * lax.rsqrt(var + GN_EPS) * lng_ref[...] + lnb_ref[...]
    bonus = _dot_hilo(r * k_h * rk_ref[...], hsum) * v
    o_ref[0, r0:r0 + rows] = ((yn + bonus) * gate).astype(o_ref.dtype)


def _rwkv(pa, mu, w0, w2, a0, a2, g2, k_k, k_a, r_k, lnx_g, lnx_b, *, rows=RW_ROWS):
    b, t, ap = pa.shape
    rows = min(rows, t)
    assert rows % RW_ROWS == 0
    width = w0.shape[0]
    w_lora, a_lora = w2.shape[0], a2.shape[0]
    assert w_lora == HEAD and a_lora == HEAD and width % LANES == 0
    n_pairs = width // LANES
    wwa = jnp.zeros((w_lora + a_lora, 2 * width), F32)
    wwa = wwa.at[:w_lora, :width].set(w2).at[w_lora:, width:].set(a2)
    ti = jnp.arange(RW_ROWS)
    tri = ((ti[None, :] <= ti[:, None])
           & (ti[None, :] // CHUNK == ti[:, None] // CHUNK)).astype(BF16)
    hi = jnp.arange(width) // HEAD
    hsum = (hi[:, None] == hi[None, :]).astype(BF16)
    vec = lambda z: z.reshape(1, width)
    return pl.pallas_call(
        _rwkv_body,
        grid=(b, t // rows),
        in_specs=[
            pl.BlockSpec((1, rows, ap), lambda i, c: (i, c, 0)),
            _const_spec((1, ap)),
            _const_spec((1, width)), _const_spec((1, width)), _const_spec((1, width)),
            _const_spec((1, width)), _const_spec((1, width)), _const_spec((1, width)),
            _const_spec((1, width)),
            _const_spec((w_lora + a_lora, 2 * width)),
            _const_spec(g2.shape),
            _const_spec((RW_ROWS, RW_ROWS)),
            _const_spec((width, width)),
        ],
        out_specs=pl.BlockSpec((1, rows, width), lambda i, c: (i, c, 0)),
        out_shape=jax.ShapeDtypeStruct((b, t, width), BF16),
        scratch_shapes=[pltpu.VMEM((1, ap), F32), pltpu.VMEM((n_pairs, LANES, LANES), F32)],
        compiler_params=_params(2),
        name="rwkv7",
    )(pa, mu.reshape(1, ap), vec(w0), vec(a0), vec(k_k), vec(k_a), vec(r_k), vec(lnx_g),
      vec(lnx_b), _bf(wwa), _bf(g2), tri, hsum)


def _rglru_body(xb_ref, gb_ref, cw_ref, cb_ref, wg_ref, bg_ref, lam_ref, o_ref,
                prev_scr, h_scr):
    i = pl.program_id(1)
    tt, width = o_ref.shape[1], o_ref.shape[2]

    @pl.when(i == 0)
    def _():
        prev_scr[...] = jnp.zeros_like(prev_scr)
        h_scr[...] = jnp.zeros_like(h_scr)

    x = xb_ref[0]
    prev = prev_scr[...]
    prev_scr[...] = x[tt - 8:, :]
    row8 = lax.broadcasted_iota(jnp.int32, (8, width), 0)

    def delayed(d):
        rolled = pltpu.roll(x, d, axis=0)
        head = jnp.where(row8 < d, pltpu.roll(prev, d, axis=0), rolled[:8])
        return jnp.concatenate([head, rolled[8:]], axis=0)

    xc = cb_ref[...] + cw_ref[CONV_WIDTH - 1:CONV_WIDTH, :] * x
    for d in range(1, CONV_WIDTH):
        xc = xc + cw_ref[CONV_WIDTH - 1 - d:CONV_WIDTH - d, :] * delayed(d)

    gates = _dot(_bf(xc), wg_ref[...]) + bg_ref[...]
    r_g = _sigmoid(gates[:, :width])
    i_g = _sigmoid(gates[:, width:])
    log_a = -LRU_C * r_g * _softplus(-lam_ref[...])
    a = jnp.exp(log_a)
    mult = jnp.sqrt(jnp.tanh(-log_a) * (1.0 + a * a))
    u = mult * i_g * xc

    a = a.reshape(tt // SUBLANES, SUBLANES, width)
    u = u.reshape(tt // SUBLANES, SUBLANES, width)
    sub = lax.broadcasted_iota(jnp.int32, a.shape, 1)
    d = 1
    while d < SUBLANES:
        keep = sub >= d
        a_sh = jnp.where(keep, pltpu.roll(a, d, axis=1), 1.0)
        u_sh = jnp.where(keep, pltpu.roll(u, d, axis=1), 0.0)
        u = a * u_sh + u
        a = a * a_sh
        d *= 2
    carry = h_scr[...]
    groups = []
    for g in range(tt // SUBLANES):
        hg = u[g] + a[g] * carry
        groups.append(hg)
        carry = hg[SUBLANES - 1:]
    h = jnp.concatenate(groups, axis=0)
    h_scr[...] = carry
    o_ref[0] = (h * _gelu_tanh(gb_ref[0])).astype(o_ref.dtype)


def _rglru(xb, gb, conv_w, conv_b, gate_a_w, gate_a_b, gate_x_w, gate_x_b, lam, *, tt=512):
    b, t, width = xb.shape
    tt = min(tt, t)
    nb, bd, _ = gate_a_w.shape
    eye = jnp.eye(nb, dtype=F32)
    dense = lambda w: jnp.einsum("nij,nm->nimj", w, eye).reshape(width, width)
    wg = _bf(jnp.concatenate([dense(gate_a_w), dense(gate_x_w)], axis=1))
    bg = jnp.concatenate([gate_a_b, gate_x_b]).reshape(1, 2 * width)
    tile = pl.BlockSpec((1, tt, width), lambda i, j: (i, j, 0))
    return pl.pallas_call(
        _rglru_body,
        grid=(b, t // tt),
        in_specs=[tile, tile, _const_spec((CONV_WIDTH, width)), _const_spec((1, width)),
                  _const_spec((width, 2 * width)), _const_spec((1, 2 * width)),
                  _const_spec((1, width))],
        out_specs=tile,
        out_shape=jax.ShapeDtypeStruct((b, t, width), BF16),
        scratch_shapes=[pltpu.VMEM((8, width), F32), pltpu.VMEM((1, width), F32)],
        compiler_params=_params(2),
        name="rglru",
    )(xb, gb, conv_w, conv_b.reshape(1, width), wg, bg, lam.reshape(1, width))


SB_DEPTH = 5
SB_ROWS = 64


def _sb_body(qkv_ref, later_ref, o_ref, zn_scr, cs_scr, lsig_scr, negb_scr, att_scr, acc_scr,
             run_scr, *, blk):
    n_p = qkv_ref.shape[1]
    n_q = qkv_ref.shape[2] // blk
    n_steps = n_p * n_q * (n_q + 1) // 2 + SB_DEPTH - 1
    first = lax.broadcasted_iota(jnp.int32, (blk, LANES), 1) < HEAD
    later = later_ref[...]
    rr = lax.broadcasted_iota(jnp.int32, (2 * blk, blk), 0)
    cc = lax.broadcasted_iota(jnp.int32, (2 * blk, blk), 1)
    ahead = cc - jnp.where(rr < blk, rr, rr - blk)

    def rows(which, p, i):
        return qkv_ref[which, p, pl.ds(pl.multiple_of(i * blk, blk), blk), :]

    def step(cur, triples):
        old = 1 - cur
        ps, qis, js = triples
        p0 = jnp.minimum(ps[0], n_p - 1)
        qa = rows(0, p0, jnp.minimum(qis[0], n_q - 1))
        q2 = jnp.concatenate([jnp.where(first, qa, 0), jnp.where(first, 0, qa)], axis=0)
        zn_scr[cur] = _dot_nt(q2, rows(1, p0, jnp.minimum(js[0], n_q - 1)))
        cs_scr[cur] = _dot(negb_scr[old], later)
        pv = _dot(att_scr[old], rows(2, ps[4], js[4]))
        for r in range(0, 2 * blk, SB_ROWS):
            rs = slice(r, r + SB_ROWS)
            att_scr[cur, rs] = _bf(jnp.exp2(lsig_scr[cur, rs] + cs_scr[old, rs]))
        for r in range(0, 2 * blk, SB_ROWS):
            rs = slice(r, r + SB_ROWS)
            znm = jnp.where(ahead[rs] < (qis[1] - js[1]) * blk, zn_scr[old, rs], 1e30)
            low = jnp.minimum(znm, 0.0)
            top = low - znm
            lse = jnp.log(1.0 + jnp.exp2(low + top)) * LOG2E
            neg = low - lse
            run = jnp.where(js[1] == qis[1], 0.0, run_scr[rs])
            negb_scr[cur, rs] = _bf(neg)
            lsig_scr[cur, rs] = (top - lse) + run
            run_scr[rs] = run + jnp.sum(neg, axis=-1, keepdims=True)
        acc = jnp.where(js[4] == qis[4], 0.0, acc_scr[...]) + pv
        acc_scr[...] = acc
        o_ref[ps[4], pl.ds(pl.multiple_of(qis[4] * blk, blk), blk), :] = jnp.where(
            first, acc[:blk], acc[blk:]).astype(o_ref.dtype)
        blk_done = js[0] == 0
        pair_done = jnp.logical_and(blk_done, qis[0] == n_q - 1)
        q_new = jnp.where(pair_done, 0, qis[0] + 1)
        p_next = jnp.where(pair_done, ps[0] + 1, ps[0])
        q_next = jnp.where(blk_done, q_new, qis[0])
        j_next = jnp.where(blk_done, q_new, js[0] - 1)
        return (p_next,) + ps[:-1], (q_next,) + qis[:-1], (j_next,) + js[:-1]

    for scr in (zn_scr, cs_scr, lsig_scr, negb_scr, att_scr, acc_scr, run_scr):
        scr[...] = jnp.zeros_like(scr)
    triples = ((jnp.int32(0),) * SB_DEPTH,) * 3
    if n_steps % 2:
        triples = step(1, triples)
    lax.fori_loop(0, n_steps // 2, lambda _, tr: step(1, step(0, tr)), triples)


def _stick_breaking(qkv, batch, *, blk=256):
    _, n_p, m, _ = qkv.shape
    t = m // batch
    blk = min(blk, t)
    ki = jnp.arange(blk)
    later = (ki[:, None] > ki[None, :]).astype(BF16)
    wide = lambda dt: pltpu.VMEM((2, 2 * blk, blk), dt)
    return pl.pallas_call(
        functools.partial(_sb_body, blk=blk),
        grid=(batch,),
        in_specs=[pl.BlockSpec((3, n_p, t, LANES), lambda i: (0, 0, i, 0)),
                  _const_spec((blk, blk))],
        out_specs=pl.BlockSpec((n_p, t, LANES), lambda i: (0, i, 0)),
        out_shape=jax.ShapeDtypeStruct((n_p, m, LANES), BF16),
        scratch_shapes=[wide(F32), wide(F32), wide(F32), wide(BF16), wide(BF16),
                        pltpu.VMEM((2 * blk, LANES), F32), pltpu.VMEM((2 * blk, 1), F32)],
        compiler_params=_params(1),
        name="stick_breaking",
    )(qkv, later)


def kernel(x, l0_ffn1_pre_g, l0_ffn1_post_g, l0_ffn1_w_in, l0_ffn1_w_out, l0_mix_pre_g, l0_mix_post_g, l0_w_in, l0_mu, l0_w0, l0_w2, l0_a0, l0_a2, l0_g2, l0_k_k, l0_k_a, l0_r_k, l0_lnx_g, l0_lnx_b, l0_conv_w, l0_conv_b, l0_gate_a_w, l0_gate_a_b, l0_gate_x_w, l0_gate_x_b, l0_lambda, l0_w_out, l0_ffn2_pre_g, l0_ffn2_post_g, l0_ffn2_w_in, l0_ffn2_w_out, l1_ffn1_pre_g, l1_ffn1_post_g, l1_ffn1_w_in, l1_ffn1_w_out, l1_mix_pre_g, l1_mix_post_g, l1_w_qkv, l1_w_out, l1_ffn2_pre_g, l1_ffn2_post_g, l1_ffn2_w_in, l1_ffn2_w_out):
    b, t, d = x.shape
    a_width = l0_w0.shape[0]
    b_width = l0_lambda.shape[0]
    a_proj = l0_mu.shape[0]
    h = x.reshape(b * t, d)

    h = _ffn(h, l0_ffn1_pre_g, l0_ffn1_post_g, l0_ffn1_w_in, l0_ffn1_w_out)
    pa, xb, gb = _proj(h, l0_mix_pre_g, l0_w_in, (a_proj, b_width, b_width), F32)
    ya = _rwkv(pa.reshape(b, t, a_proj), l0_mu, l0_w0, l0_w2, l0_a0, l0_a2, l0_g2, l0_k_k,
               l0_k_a, l0_r_k.reshape(-1), l0_lnx_g, l0_lnx_b)
    yb = _rglru(xb.reshape(b, t, b_width), gb.reshape(b, t, b_width), l0_conv_w, l0_conv_b,
                l0_gate_a_w, l0_gate_a_b, l0_gate_x_w, l0_gate_x_b, l0_lambda)
    h = _ffn(h, l0_ffn2_pre_g, l0_ffn2_post_g, l0_ffn2_w_in, l0_ffn2_w_out,
             mix=([ya.reshape(b * t, a_width), yb.reshape(b * t, b_width)], l0_w_out,
                  l0_mix_post_g))

    h = _ffn(h, l1_ffn1_pre_g, l1_ffn1_post_g, l1_ffn1_w_in, l1_ffn1_w_out)
    qkv = _proj_tiles(h, l1_mix_pre_g, l1_w_qkv, 3, BF16,
                      scales=(-LOG2E / math.sqrt(HEAD), None, None))
    o = _stick_breaking(qkv, b)
    h = _ffn(h, l1_ffn2_pre_g, l1_ffn2_post_g, l1_ffn2_w_in, l1_ffn2_w_out,
             mix=([o], l1_w_out, l1_mix_post_g))
    return h.reshape(b, t, d)
```

```python
import functools
import math

import jax
import jax.numpy as jnp
from jax import lax
from jax.experimental import pallas as pl
from jax.experimental.pallas import tpu as pltpu

F32 = jnp.float32
BF16 = jnp.bfloat16

NORM_EPS = 1e-6
GN_EPS = 64e-5
LRU_C = 8.0
HEAD = 64
LANES = 128
SUBLANES = 8
CHUNK = 64
RW_ROWS = 256
CONV_WIDTH = 4
LOG2E = 1.0 / math.log(2.0)
VMEM_LIMIT = 56 * 1024 * 1024


def _params(n_axes, vmem=VMEM_LIMIT):
    return pltpu.CompilerParams(
        dimension_semantics=("arbitrary",) * n_axes, vmem_limit_bytes=vmem)


def _bf(x):
    return x.astype(BF16)


def _dot(a, b):
    return jnp.dot(a, b, preferred_element_type=F32)


def _dot_nt(a, b):
    return lax.dot_general(a, b, (((1,), (1,)), ((), ())), preferred_element_type=F32)


def _dot_hilo(x, e):
    hi = _bf(x)
    lo = _bf(x - hi.astype(F32))
    return _dot(hi, e) + _dot(lo, e)


def _rms(x, g):
    return x * lax.rsqrt(jnp.mean(x * x, axis=-1, keepdims=True) + NORM_EPS) * g


def _sigmoid(x):
    return 1.0 / (1.0 + jnp.exp(-x))


def _softplus(x):
    return jnp.maximum(x, 0.0) + jnp.log1p(jnp.exp(-jnp.abs(x)))


def _gelu_tanh(x):
    c = math.sqrt(2.0 / math.pi)
    return 0.5 * x * (1.0 + jnp.tanh(c * (x + 0.044715 * (x * x * x))))


def _row_spec(tm, n):
    return pl.BlockSpec((tm, n), lambda i, *_: (i, 0))


def _const_spec(shape):
    return pl.BlockSpec(shape, lambda *_: (0,) * len(shape))


def _act_rows(a_ref, rs):
    if len(a_ref.shape) == 2:
        return a_ref[rs, :]
    return jnp.concatenate([a_ref[p, rs, :] for p in range(a_ref.shape[0])], axis=1)


def _ffn_body(*refs, tf, n_mix, n_sub):
    a_refs, w_refs = refs[:n_mix], refs[n_mix:2 * n_mix]
    rest = refs[2 * n_mix:]
    if n_mix:
        gmix_ref, rest = rest[0], rest[1:]
    x_ref, gpre_ref, gpost_ref, win_ref, wo_ref, o_ref = rest
    f = wo_ref.shape[0]
    sub = x_ref.shape[0] // n_sub
    spans = [slice(s * sub, (s + 1) * sub) for s in range(n_sub)]
    xs = [x_ref[rs, :] for rs in spans]
    if n_mix:
        ys = [_dot(_act_rows(a_refs[0], rs), w_refs[0][...]) for rs in spans]
        for a_ref, w_ref in zip(a_refs[1:], w_refs[1:]):
            ys = [y + _dot(_act_rows(a_ref, rs), w_ref[...]) for y, rs in zip(ys, spans)]
        xs = [x + _rms(y, gmix_ref[...]) for x, y in zip(xs, ys)]
    hs = [_bf(_rms(x, gpre_ref[...])) for x in xs]

    def gate_up(h, c):
        return (_dot(h, win_ref[:, c * tf:(c + 1) * tf]),
                _dot(h, win_ref[:, f + c * tf:f + (c + 1) * tf]))

    n = f // tf
    accs = [None] * n_sub
    gus = [gate_up(h, 0) for h in hs]
    for c in range(n):
        for s in range(n_sub):
            gate, up = gus[s]
            act = _bf(gate * _sigmoid(gate) * up)
            if c + 1 < n:
                gus[s] = gate_up(hs[s], c + 1)
            part = _dot(act, wo_ref[c * tf:(c + 1) * tf, :])
            accs[s] = part if accs[s] is None else accs[s] + part
    for rs, x, acc in zip(spans, xs, accs):
        o_ref[rs, :] = x + 0.5 * _rms(acc, gpost_ref[...])


def _ffn(x, g_pre, g_post, w_in, w_out, *, mix=None, tm=1024, n_sub=4, tf=256):
    m, d = x.shape
    tm = min(tm, m)
    f = w_out.shape[0]
    acts, ws, g_mix = mix if mix is not None else ((), (), None)
    resident = lambda shape: pl.BlockSpec(shape, lambda i: (0, 0), pipeline_mode=pl.Buffered(1))
    act_spec = lambda a: (_row_spec(tm, a.shape[1]) if a.ndim == 2 else
                          pl.BlockSpec((a.shape[0], tm, a.shape[2]), lambda i: (0, i, 0)))
    mix_specs = [act_spec(a) for a in acts] + [resident(w.shape) for w in ws]
    mix_args = list(acts) + [_bf(w) for w in ws]
    if mix is not None:
        mix_specs.append(_const_spec((1, d)))
        mix_args.append(g_mix.reshape(1, d))
    return pl.pallas_call(
        functools.partial(_ffn_body, tf=tf, n_mix=len(acts), n_sub=n_sub),
        grid=(m // tm,),
        in_specs=mix_specs + [_row_spec(tm, d), _const_spec((1, d)), _const_spec((1, d)),
                              resident((d, 2 * f)), resident((f, d))],
        out_specs=_row_spec(tm, d),
        out_shape=jax.ShapeDtypeStruct((m, d), F32),
        compiler_params=_params(1),
        name="ffn",
    )(*mix_args, x, g_pre.reshape(1, d), g_post.reshape(1, d), _bf(w_in), _bf(w_out))


def _proj_body(x_ref, g_ref, w_ref, *o_refs, scales):
    h = _bf(_rms(x_ref[...], g_ref[...]))
    y = _dot(h, w_ref[...])
    off = 0
    for o_ref, scale in zip(o_refs, scales):
        n = o_ref.shape[-1]
        part = y[:, off:off + n]
        if scale is not None:
            part = part * scale
        o_ref[...] = part.astype(o_ref.dtype)
        off += n


def _proj(x, g, w, splits, dtype, *, scales=None, tm=512):
    m, d = x.shape
    tm = min(tm, m)
    n = w.shape[1]
    assert sum(splits) == n
    scales = scales or (None,) * len(splits)
    return pl.pallas_call(
        functools.partial(_proj_body, scales=scales),
        grid=(m // tm,),
        in_specs=[_row_spec(tm, d), _const_spec((1, d)), _const_spec((d, n))],
        out_specs=[_row_spec(tm, s) for s in splits],
        out_shape=[jax.ShapeDtypeStruct((m, s), dtype) for s in splits],
        compiler_params=_params(1),
        name="proj",
    )(x, g.reshape(1, d), _bf(w))


def _proj_tiles_body(x_ref, g_ref, w_ref, o_ref, *, scales):
    h = _bf(_rms(x_ref[...], g_ref[...]))
    y = _dot(h, w_ref[...])
    n_groups, n_tiles = o_ref.shape[0], o_ref.shape[1]
    for s in range(n_groups):
        for p in range(n_tiles):
            off = (s * n_tiles + p) * LANES
            part = y[:, off:off + LANES]
            if scales[s] is not None:
                part = part * scales[s]
            o_ref[s, p] = part.astype(o_ref.dtype)


def _proj_tiles(x, g, w, n_groups, dtype, *, scales, tm=512):
    m, d = x.shape
    tm = min(tm, m)
    n = w.shape[1]
    n_tiles = n // (n_groups * LANES)
    return pl.pallas_call(
        functools.partial(_proj_tiles_body, scales=scales),
        grid=(m // tm,),
        in_specs=[_row_spec(tm, d), _const_spec((1, d)), _const_spec((d, n))],
        out_specs=pl.BlockSpec((n_groups, n_tiles, tm, LANES), lambda i: (0, 0, i, 0)),
        out_shape=jax.ShapeDtypeStruct((n_groups, n_tiles, m, LANES), dtype),
        compiler_params=_params(1),
        name="proj_tiles",
    )(x, g.reshape(1, d), _bf(w))


def _solve_unit_lower(ns, xs):
    ps = [_bf(n) for n in ns]
    steps = int(math.log2(CHUNK))
    for i in range(steps):
        xs = [x + _dot(p, _bf(x)) for p, x in zip(ps, xs)]
        if i + 1 < steps:
            ps = [_bf(_dot(p, p)) for p in ps]
    return xs


def _rwkv_body(pa_ref, mu_ref, w0_ref, a0_ref, kk_ref, ka_ref, rk_ref, lng_ref, lnb_ref,
               wwa_ref, g2_ref, tri_ref, hsum_ref, o_ref, prev_scr, s_scr):
    @pl.when(pl.program_id(1) == 0)
    def _():
        prev_scr[...] = jnp.zeros_like(prev_scr)
        s_scr[...] = jnp.zeros_like(s_scr)

    for r0 in range(0, o_ref.shape[1], RW_ROWS):
        _rwkv_rows(r0, pa_ref, mu_ref, w0_ref, a0_ref, kk_ref, ka_ref, rk_ref, lng_ref, lnb_ref,
                   wwa_ref, g2_ref, tri_ref, hsum_ref, o_ref, prev_scr, s_scr)


def _rwkv_rows(r0, pa_ref, mu_ref, w0_ref, a0_ref, kk_ref, ka_ref, rk_ref, lng_ref, lnb_ref,
               wwa_ref, g2_ref, tri_ref, hsum_ref, o_ref, prev_scr, s_scr):
    rows, width = RW_ROWS, o_ref.shape[2]
    n_pairs = width // LANES
    n_chunks = rows // CHUNK
    p = pa_ref[0, r0:r0 + rows]
    row = lax.broadcasted_iota(jnp.int32, p.shape, 0)
    shifted = jnp.where(row == 0, prev_scr[...], pltpu.roll(p, 1, axis=0))
    prev_scr[...] = p[rows - 1:rows, :]
    xm = p + mu_ref[...] * (shifted - p)

    r = xm[:, 0:width]
    k = xm[:, width:2 * width]
    v = xm[:, 2 * width:3 * width]
    wa_lo = xm[:, 3 * width:3 * width + LANES]
    g_lo = xm[:, 3 * width + LANES:]
    lane_wa = lax.broadcasted_iota(jnp.int32, wa_lo.shape, 1)
    wa_act = jnp.where(lane_wa < LANES // 2, jnp.tanh(wa_lo), wa_lo)
    lora = _dot(_bf(wa_act), wwa_ref[...])
    w_log = -_softplus(-(w0_ref[...] + lora[:, :width])) - 0.5
    log_decay = -jnp.exp(w_log)
    a = _sigmoid(a0_ref[...] + lora[:, width:])
    gate = _dot(_bf(_sigmoid(g_lo)), g2_ref[...])

    hsum = hsum_ref[...]
    kk = k * kk_ref[...]
    kk = kk / jnp.maximum(jnp.sqrt(_dot_hilo(kk * kk, hsum)), 1e-12)
    k_h = k * (1.0 + (a - 1.0) * ka_ref[...])

    ld_hi = _bf(log_decay)
    ld_lo = _bf(log_decay - ld_hi.astype(F32))
    tri = tri_ref[...]
    cum = _dot(tri, ld_hi) + _dot(tri, ld_lo)
    dec_in = jnp.exp(cum)
    dec_out = jnp.exp(-cum)
    r_f = r * dec_in
    a_f = -kk * jnp.exp(cum - log_decay)
    b_t = _bf(kk * a * dec_out)
    k_t = _bf(k_h * dec_out)

    lane = lax.broadcasted_iota(jnp.int32, (CHUNK, LANES), 1)
    trow = lax.broadcasted_iota(jnp.int32, (CHUNK, LANES), 0)
    first = lane < HEAD
    src = jnp.where(first, lane, lane - HEAD)
    strict = src < trow
    incl = src <= trow
    lane2 = lax.broadcasted_iota(jnp.int32, (CHUNK, 2 * LANES), 1)
    first2 = jnp.where(lane2 < LANES, lane2, lane2 - LANES) < HEAD
    sr = lax.broadcasted_iota(jnp.int32, (LANES, LANES), 0)
    sc = lax.broadcasted_iota(jnp.int32, (LANES, LANES), 1)
    same_head = (sr < HEAD) == (sc < HEAD)
    z16 = jnp.zeros((CHUNK, LANES), BF16)
    z32 = jnp.zeros((CHUNK, LANES), F32)

    inst = [(c, pr) for c in range(n_chunks) for pr in range(n_pairs)]
    blk = lambda z, c, pr: z[c * CHUNK:(c + 1) * CHUNK, pr * LANES:(pr + 1) * LANES]
    afs = [blk(a_f, c, pr) for c, pr in inst]
    rfs = [blk(r_f, c, pr) for c, pr in inst]
    vfs = [blk(v, c, pr) for c, pr in inst]
    vts = [_bf(vf) for vf in vfs]
    bks = [jnp.concatenate([blk(b_t, c, pr), blk(k_t, c, pr)], axis=0) for c, pr in inst]
    kbs = [jnp.concatenate([blk(k_t, c, pr), blk(b_t, c, pr)], axis=0) for c, pr in inst]
    gh0s = [_dot_nt(jnp.concatenate([jnp.where(first, _bf(af), 0), jnp.where(first, _bf(rf), 0)],
                                    axis=0), bk) for af, rf, bk in zip(afs, rfs, bks)]
    gh1s = [_dot_nt(jnp.concatenate([jnp.where(first, 0, _bf(af)), jnp.where(first, 0, _bf(rf))],
                                    axis=0), kb) for af, rf, kb in zip(afs, rfs, kbs)]
    g0s = [jnp.where(strict, gh[:CHUNK], 0.0) for gh in gh0s]
    g1s = [jnp.where(strict, gh[:CHUNK], 0.0) for gh in gh1s]
    lhss = [jnp.concatenate(
        [jnp.concatenate([_bf(jnp.where(incl, ga[CHUNK:], 0.0)), z16], axis=1),
         jnp.concatenate([z16, _bf(jnp.where(incl, gb[CHUNK:], 0.0))], axis=1)], axis=0)
        for ga, gb in zip(gh0s, gh1s)]
    n_blks = [jnp.concatenate([jnp.where(first, g0, 0.0), jnp.where(first, 0.0, g1)], axis=0)
              for g0, g1 in zip(g0s, g1s)]
    akvs = [_dot(_bf(jnp.concatenate([jnp.where(first, 0.0, g0), jnp.where(first, g1, 0.0)],
                                     axis=0)), jnp.concatenate([vt, vt], axis=0))
            for g0, g1, vt in zip(g0s, g1s, vts)]
    xs = [jnp.concatenate([jnp.concatenate([af, af], axis=0), akv], axis=1)
          for af, akv in zip(afs, akvs)]
    xs = _solve_unit_lower(n_blks, xs)
    lows = [jnp.concatenate([jnp.where(first2, x[:CHUNK], x[CHUNK:]),
                             jnp.concatenate([z32, vf], axis=1)], axis=0)
            for x, vf in zip(xs, vfs)]
    lowbs = [_bf(low) for low in lows]
    outs = [_dot(lhs, jnp.concatenate([lowb, lowb[CHUNK:], lowb[:CHUNK]], axis=0))
            for lhs, lowb in zip(lhss, lowbs)]
    sels = [jnp.where(first2, out[:CHUNK], out[CHUNK:]) for out in outs]
    mcs = [_dot(_bf(low.T), bk) for low, bk in zip(lows, bks)]

    states = [s_scr[pr] for pr in range(n_pairs)]
    y_rows = []
    for c in range(n_chunks):
        y_pairs = []
        for pr in range(n_pairs):
            i = c * n_pairs + pr
            w_last = dec_in[(c + 1) * CHUNK - 1:(c + 1) * CHUNK, pr * LANES:(pr + 1) * LANES]
            g_tilde = rfs[i] + sels[i][:, :LANES]
            y_intra = sels[i][:, LANES:]
            m_lr = jnp.where(same_head, mcs[i][:LANES], 0.0) * w_last
            c_mat = jnp.where(same_head, mcs[i][LANES:], 0.0) * w_last
            s = states[pr]
            sb = _bf(s)
            y_pairs.append(_dot_nt(_bf(g_tilde), sb) + y_intra)
            states[pr] = s * w_last + _dot(sb, _bf(m_lr)) + c_mat
        y_rows.append(jnp.concatenate(y_pairs, axis=-1))
    for pr in range(n_pairs):
        s_scr[pr] = states[pr]
    y = jnp.concatenate(y_rows, axis=0)

    inv_n = 1.0 / HEAD
    mean = _dot_hilo(y, hsum) * inv_n
    dev = y - mean
    var = _dot_hilo(dev * dev, hsum) * inv_n
    yn = dev * lax.rsqrt(var + GN_EPS) * lng_ref[...] + lnb_ref[...]
    bonus = _dot_hilo(r * k_h * rk_ref[...], hsum) * v
    o_ref[0, r0:r0 + rows] = ((yn + bonus) * gate).astype(o_ref.dtype)


def _rwkv(pa, mu, w0, w2, a0, a2, g2, k_k, k_a, r_k, lnx_g, lnx_b, *, rows=RW_ROWS):
    b, t, ap = pa.shape
    rows = min(rows, t)
    assert rows % RW_ROWS == 0
    width = w0.shape[0]
    w_lora, a_lora = w2.shape[0], a2.shape[0]
    assert w_lora == HEAD and a_lora == HEAD and width % LANES == 0
    n_pairs = width // LANES
    wwa = jnp.zeros((w_lora + a_lora, 2 * width), F32)
    wwa = wwa.at[:w_lora, :width].set(w2).at[w_lora:, width:].set(a2)
    ti = jnp.arange(RW_ROWS)
    tri = ((ti[None, :] <= ti[:, None])
           & (ti[None, :] // CHUNK == ti[:, None] // CHUNK)).astype(BF16)
    hi = jnp.arange(width) // HEAD
    hsum = (hi[:, None] == hi[None, :]).astype(BF16)
    vec = lambda z: z.reshape(1, width)
    return pl.pallas_call(
        _rwkv_body,
        grid=(b, t // rows),
        in_specs=[
            pl.BlockSpec((1, rows, ap), lambda i, c: (i, c, 0)),
            _const_spec((1, ap)),
            _const_spec((1, width)), _const_spec((1, width)), _const_spec((1, width)),
            _const_spec((1, width)), _const_spec((1, width)), _const_spec((1, width)),
            _const_spec((1, width)),
            _const_spec((w_lora + a_lora, 2 * width)),
            _const_spec(g2.shape),
            _const_spec((RW_ROWS, RW_ROWS)),
            _const_spec((width, width)),
        ],
        out_specs=pl.BlockSpec((1, rows, width), lambda i, c: (i, c, 0)),
        out_shape=jax.ShapeDtypeStruct((b, t, width), BF16),
        scratch_shapes=[pltpu.VMEM((1, ap), F32), pltpu.VMEM((n_pairs, LANES, LANES), F32)],
        compiler_params=_params(2),
        name="rwkv7",
    )(pa, mu.reshape(1, ap), vec(w0), vec(a0), vec(k_k), vec(k_a), vec(r_k), vec(lnx_g),
      vec(lnx_b), _bf(wwa), _bf(g2), tri, hsum)


def _rglru_body(xb_ref, gb_ref, cw_ref, cb_ref, wg_ref, bg_ref, lam_ref, o_ref,
                prev_scr, h_scr):
    i = pl.program_id(1)
    tt, width = o_ref.shape[1], o_ref.shape[2]

    @pl.when(i == 0)
    def _():
        prev_scr[...] = jnp.zeros_like(prev_scr)
        h_scr[...] = jnp.zeros_like(h_scr)

    x = xb_ref[0]
    prev = prev_scr[...]
    prev_scr[...] = x[tt - 8:, :]
    row8 = lax.broadcasted_iota(jnp.int32, (8, width), 0)

    def delayed(d):
        rolled = pltpu.roll(x, d, axis=0)
        head = jnp.where(row8 < d, pltpu.roll(prev, d, axis=0), rolled[:8])
        return jnp.concatenate([head, rolled[8:]], axis=0)

    xc = cb_ref[...] + cw_ref[CONV_WIDTH - 1:CONV_WIDTH, :] * x
    for d in range(1, CONV_WIDTH):
        xc = xc + cw_ref[CONV_WIDTH - 1 - d:CONV_WIDTH - d, :] * delayed(d)

    gates = _dot(_bf(xc), wg_ref[...]) + bg_ref[...]
    r_g = _sigmoid(gates[:, :width])
    i_g = _sigmoid(gates[:, width:])
    log_a = -LRU_C * r_g * _softplus(-lam_ref[...])
    a = jnp.exp(log_a)
    mult = jnp.sqrt(jnp.tanh(-log_a) * (1.0 + a * a))
    u = mult * i_g * xc

    a = a.reshape(tt // SUBLANES, SUBLANES, width)
    u = u.reshape(tt // SUBLANES, SUBLANES, width)
    sub = lax.broadcasted_iota(jnp.int32, a.shape, 1)
    d = 1
    while d < SUBLANES:
        keep = sub >= d
        a_sh = jnp.where(keep, pltpu.roll(a, d, axis=1), 1.0)
        u_sh = jnp.where(keep, pltpu.roll(u, d, axis=1), 0.0)
        u = a * u_sh + u
        a = a * a_sh
        d *= 2
    carry = h_scr[...]
    groups = []
    for g in range(tt // SUBLANES):
        hg = u[g] + a[g] * carry
        groups.append(hg)
        carry = hg[SUBLANES - 1:]
    h = jnp.concatenate(groups, axis=0)
    h_scr[...] = carry
    o_ref[0] = (h * _gelu_tanh(gb_ref[0])).astype(o_ref.dtype)


def _rglru(xb, gb, conv_w, conv_b, gate_a_w, gate_a_b, gate_x_w, gate_x_b, lam, *, tt=256):
    b, t, width = xb.shape
    tt = min(tt, t)
    nb, bd, _ = gate_a_w.shape
    eye = jnp.eye(nb, dtype=F32)
    dense = lambda w: jnp.einsum("nij,nm->nimj", w, eye).reshape(width, width)
    wg = _bf(jnp.concatenate([dense(gate_a_w), dense(gate_x_w)], axis=1))
    bg = jnp.concatenate([gate_a_b, gate_x_b]).reshape(1, 2 * width)
    tile = pl.BlockSpec((1, tt, width), lambda i, j: (i, j, 0))
    return pl.pallas_call(
        _rglru_body,
        grid=(b, t // tt),
        in_specs=[tile, tile, _const_spec((CONV_WIDTH, width)), _const_spec((1, width)),
                  _const_spec((width, 2 * width)), _const_spec((1, 2 * width)),
                  _const_spec((1, width))],
        out_specs=tile,
        out_shape=jax.ShapeDtypeStruct((b, t, width), BF16),
        scratch_shapes=[pltpu.VMEM((8, width), F32), pltpu.VMEM((1, width), F32)],
        compiler_params=_params(2),
        name="rglru",
    )(xb, gb, conv_w, conv_b.reshape(1, width), wg, bg, lam.reshape(1, width))


SB_DEPTH = 5
SB_ROWS = 64


def _sb_body(qkv_ref, later_ref, o_ref, zn_scr, cs_scr, lsig_scr, negb_scr, att_scr, acc_scr,
             run_scr, *, blk):
    n_p = qkv_ref.shape[1]
    n_q = qkv_ref.shape[2] // blk
    n_steps = n_p * n_q * (n_q + 1) // 2 + SB_DEPTH - 1
    first = lax.broadcasted_iota(jnp.int32, (blk, LANES), 1) < HEAD
    later = later_ref[...]
    rr = lax.broadcasted_iota(jnp.int32, (2 * blk, blk), 0)
    cc = lax.broadcasted_iota(jnp.int32, (2 * blk, blk), 1)
    ahead = cc - jnp.where(rr < blk, rr, rr - blk)

    def rows(which, p, i):
        return qkv_ref[which, p, pl.ds(pl.multiple_of(i * blk, blk), blk), :]

    def step(cur, triples):
        old = 1 - cur
        ps, qis, js = triples
        p0 = jnp.minimum(ps[0], n_p - 1)
        qa = rows(0, p0, jnp.minimum(qis[0], n_q - 1))
        q2 = jnp.concatenate([jnp.where(first, qa, 0), jnp.where(first, 0, qa)], axis=0)
        zn_scr[cur] = _dot_nt(q2, rows(1, p0, jnp.minimum(js[0], n_q - 1)))
        cs_scr[cur] = _dot(negb_scr[old], later_ref[...])
        pv = _dot(att_scr[old], rows(2, ps[4], js[4]))
        for r in range(0, 2 * blk, SB_ROWS):
            rs = slice(r, r + SB_ROWS)
            att_scr[cur, rs] = _bf(jnp.exp2(lsig_scr[cur, rs] + cs_scr[old, rs]))
        for r in range(0, 2 * blk, SB_ROWS):
            rs = slice(r, r + SB_ROWS)
            znm = jnp.where(ahead[rs] < (qis[1] - js[1]) * blk, zn_scr[old, rs], 1e30)
            low = jnp.minimum(znm, 0.0)
            top = low - znm
            lse = jnp.log(1.0 + jnp.exp2(low + top)) * LOG2E
            neg = low - lse
            run = jnp.where(js[1] == qis[1], 0.0, run_scr[rs])
            negb_scr[cur, rs] = _bf(neg)
            lsig_scr[cur, rs] = (top - lse) + run
            run_scr[rs] = run + jnp.sum(neg, axis=-1, keepdims=True)
        acc = jnp.where(js[4] == qis[4], 0.0, acc_scr[...]) + pv
        acc_scr[...] = acc
        o_ref[ps[4], pl.ds(pl.multiple_of(qis[4] * blk, blk), blk), :] = jnp.where(
            first, acc[:blk], acc[blk:]).astype(o_ref.dtype)
        blk_done = js[0] == 0
        pair_done = jnp.logical_and(blk_done, qis[0] == n_q - 1)
        q_new = jnp.where(pair_done, 0, qis[0] + 1)
        p_next = jnp.where(pair_done, ps[0] + 1, ps[0])
        q_next = jnp.where(blk_done, q_new, qis[0])
        j_next = jnp.where(blk_done, q_new, js[0] - 1)
        return (p_next,) + ps[:-1], (q_next,) + qis[:-1], (j_next,) + js[:-1]

    for scr in (zn_scr, cs_scr, lsig_scr, negb_scr, att_scr, acc_scr, run_scr):
        scr[...] = jnp.zeros_like(scr)
    triples = ((jnp.int32(0),) * SB_DEPTH,) * 3
    if n_steps % 2:
        triples = step(1, triples)
    lax.fori_loop(0, n_steps // 2, lambda _, tr: step(1, step(0, tr)), triples)


def _stick_breaking(qkv, batch, *, blk=256):
    _, n_p, m, _ = qkv.shape
    t = m // batch
    blk = min(blk, t)
    ki = jnp.arange(blk)
    later = (ki[:, None] > ki[None, :]).astype(BF16)
    wide = lambda dt: pltpu.VMEM((2, 2 * blk, blk), dt)
    return pl.pallas_call(
        functools.partial(_sb_body, blk=blk),
        grid=(batch,),
        in_specs=[pl.BlockSpec((3, n_p, t, LANES), lambda i: (0, 0, i, 0)),
                  _const_spec((blk, blk))],
        out_specs=pl.BlockSpec((n_p, t, LANES), lambda i: (0, i, 0)),
        out_shape=jax.ShapeDtypeStruct((n_p, m, LANES), BF16),
        scratch_shapes=[wide(F32), wide(F32), wide(F32), wide(BF16), wide(BF16),
                        pltpu.VMEM((2 * blk, LANES), F32), pltpu.VMEM((2 * blk, 1), F32)],
        compiler_params=_params(1),
        name="stick_breaking",
    )(qkv, later)


def kernel(x, l0_ffn1_pre_g, l0_ffn1_post_g, l0_ffn1_w_in, l0_ffn1_w_out, l0_mix_pre_g, l0_mix_post_g, l0_w_in, l0_mu, l0_w0, l0_w2, l0_a0, l0_a2, l0_g2, l0_k_k, l0_k_a, l0_r_k, l0_lnx_g, l0_lnx_b, l0_conv_w, l0_conv_b, l0_gate_a_w, l0_gate_a_b, l0_gate_x_w, l0_gate_x_b, l0_lambda, l0_w_out, l0_ffn2_pre_g, l0_ffn2_post_g, l0_ffn2_w_in, l0_ffn2_w_out, l1_ffn1_pre_g, l1_ffn1_post_g, l1_ffn1_w_in, l1_ffn1_w_out, l1_mix_pre_g, l1_mix_post_g, l1_w_qkv, l1_w_out, l1_ffn2_pre_g, l1_ffn2_post_g, l1_ffn2_w_in, l1_ffn2_w_out):
    b, t, d = x.shape
    a_width = l0_w0.shape[0]
    b_width = l0_lambda.shape[0]
    a_proj = l0_mu.shape[0]
    h = x.reshape(b * t, d)

    h = _ffn(h, l0_ffn1_pre_g, l0_ffn1_post_g, l0_ffn1_w_in, l0_ffn1_w_out)
    pa, xb, gb = _proj(h, l0_mix_pre_g, l0_w_in, (a_proj, b_width, b_width), F32)
    ya = _rwkv(pa.reshape(b, t, a_proj), l0_mu, l0_w0, l0_w2, l0_a0, l0_a2, l0_g2, l0_k_k,
               l0_k_a, l0_r_k.reshape(-1), l0_lnx_g, l0_lnx_b)
    yb = _rglru(xb.reshape(b, t, b_width), gb.reshape(b, t, b_width), l0_conv_w, l0_conv_b,
                l0_gate_a_w, l0_gate_a_b, l0_gate_x_w, l0_gate_x_b, l0_lambda)
    h = _ffn(h, l0_ffn2_pre_g, l0_ffn2_post_g, l0_ffn2_w_in, l0_ffn2_w_out,
             mix=([ya.reshape(b * t, a_width), yb.reshape(b * t, b_width)],
                  [l0_w_out[:a_width], l0_w_out[a_width:]], l0_mix_post_g))

    h = _ffn(h, l1_ffn1_pre_g, l1_ffn1_post_g, l1_ffn1_w_in, l1_ffn1_w_out)
    qkv = _proj_tiles(h, l1_mix_pre_g, l1_w_qkv, 3, BF16,
                      scales=(-LOG2E / math.sqrt(HEAD), None, None))
    o = _stick_breaking(qkv, b)
    h = _ffn(h, l1_ffn2_pre_g, l1_ffn2_post_g, l1_ffn2_w_in, l1_ffn2_w_out,
             mix=([o], [l1_w_out], l1_mix_post_g))
    return h.reshape(b, t, d)
```
